```python
import jax, jax.numpy as jnp
from jax import lax
import numpy as np

D_MODEL = 1024
BATCH = 8
SEQ = 4096
DEPTH = 4

GRID_W = 64
CTX_LEN = 256
HEAD_DIM = 64
ATTN_DIM = D_MODEL // 2
N_Q_HEADS = ATTN_DIM // HEAD_DIM
N_KV_HEADS = N_Q_HEADS // 4
Q_PER_KV = N_Q_HEADS // N_KV_HEADS
KV_DIM = N_KV_HEADS * HEAD_DIM
WINDOW = 128
BLOCK = 128
ROPE_THETA = 10000.0
ROPE_AXIS_PAIRS = HEAD_DIM // 4
POOL_WINDOWS = (2, 4, 8, 16)
POOL_DIM = D_MODEL - ATTN_DIM
POOL_GROUP_DIM = POOL_DIM // len(POOL_WINDOWS)
AP_GATE_DIM = ATTN_DIM + POOL_DIM
AP_IN_DIM = ATTN_DIM + 2 * KV_DIM + POOL_DIM + AP_GATE_DIM
CONV_DIM = D_MODEL
CONV_K = 31
CV_IN_DIM = 3 * CONV_DIM
N_AP_LAYERS = (DEPTH + 1) // 2
N_CV_LAYERS = DEPTH // 2
EPS = 1e-6
NEG_INF = -1e30

kernel_name = "hybrid_window_gqa_pool_conformer_dit"


def _rmsnorm(x, g):
    xf = x.astype(jnp.float32)
    y = xf * lax.rsqrt(jnp.mean(xf * xf, axis=-1, keepdims=True) + EPS)
    return (y * g.astype(jnp.float32)).astype(x.dtype)


def _layernorm(x, g, b):
    xf = x.astype(jnp.float32)
    mu = jnp.mean(xf, axis=-1, keepdims=True)
    xc = xf - mu
    y = xc * lax.rsqrt(jnp.mean(xc * xc, axis=-1, keepdims=True) + EPS)
    return (y * g.astype(jnp.float32) + b.astype(jnp.float32)).astype(x.dtype)


def _modulate(x, cond, ada_w, ada_b, norm_g):
    shift, scale, gate = jnp.split(jax.nn.silu(cond) @ ada_w + ada_b, 3, axis=-1)
    h = _rmsnorm(x, norm_g) * (1 + scale) + shift
    return h, gate


def _axial_rope_tables(n_tokens):
    rows = n_tokens // GRID_W
    row = jnp.repeat(jnp.arange(rows, dtype=jnp.float32), GRID_W)
    col = jnp.tile(jnp.arange(GRID_W, dtype=jnp.float32), rows)
    inv_freq = ROPE_THETA ** (-jnp.arange(ROPE_AXIS_PAIRS, dtype=jnp.float32) / ROPE_AXIS_PAIRS)
    ang = jnp.concatenate([row[:, None] * inv_freq, col[:, None] * inv_freq], axis=-1)
    return jnp.cos(ang), jnp.sin(ang)


def _rope(x, cos, sin):
    x1, x2 = jnp.split(x.astype(jnp.float32), 2, axis=-1)
    cs, sn = cos[:, None, :], sin[:, None, :]
    return jnp.concatenate([x1 * cs - x2 * sn, x1 * sn + x2 * cs], axis=-1).astype(x.dtype)


def _heads(t, n_heads):
    return t.reshape(t.shape[:2] + (n_heads, HEAD_DIM))


def _sink_attention(q, k, v, sink, mask=None):
    s = jnp.einsum('bqhgd,bkhd->bhgqk', q, k, preferred_element_type=jnp.float32) * (HEAD_DIM ** -0.5)
    if mask is not None:
        s = jnp.where(mask, s, NEG_INF)
    sink_col = jnp.broadcast_to(sink.astype(jnp.float32).reshape(1, N_KV_HEADS, Q_PER_KV, 1, 1),
                                s.shape[:-1] + (1,))
    p = jax.nn.softmax(jnp.concatenate([s, sink_col], axis=-1), axis=-1)[..., :-1]
    return jnp.einsum('bhgqk,bkhd->bqhgd', p.astype(v.dtype), v)


def _banded_window_attention(q, k, v, kc, vc, sink):
    B, L = q.shape[:2]
    n_blocks = L // BLOCK
    qb = q.reshape(B, n_blocks, BLOCK, N_KV_HEADS, Q_PER_KV, HEAD_DIM)
    pad = ((0, 0), (BLOCK, BLOCK), (0, 0), (0, 0))
    kp, vp = jnp.pad(k, pad), jnp.pad(v, pad)
    r = jnp.arange(BLOCK)
    cidx = jnp.arange(3 * BLOCK)
    band = jnp.abs((cidx[None, :] - BLOCK) - r[:, None]) <= WINDOW
    ctx_cols = jnp.ones((BLOCK, kc.shape[1]), dtype=bool)

    def one_block(i):
        q_i = lax.dynamic_index_in_dim(qb, i, axis=1, keepdims=False)
        k_i = lax.dynamic_slice_in_dim(kp, i * BLOCK, 3 * BLOCK, axis=1)
        v_i = lax.dynamic_slice_in_dim(vp, i * BLOCK, 3 * BLOCK, axis=1)
        kpos = i * BLOCK - BLOCK + cidx
        inside = (kpos >= 0) & (kpos < L)
        mask = jnp.concatenate([band & inside[None, :], ctx_cols], axis=1)
        return _sink_attention(q_i, jnp.concatenate([k_i, kc], axis=1),
                               jnp.concatenate([v_i, vc], axis=1), sink, mask)

    o = lax.map(one_block, jnp.arange(n_blocks))
    return jnp.moveaxis(o, 0, 1).reshape(B, L, ATTN_DIM)


def _multiscale_pool(u, pool_w, pool_scale):
    T = u.shape[1]
    uf = u.astype(jnp.float32)
    cs = jnp.concatenate([jnp.zeros_like(uf[:, :1]), jnp.cumsum(uf, axis=1)], axis=1)
    t = jnp.arange(T)
    diffs = []
    for g, w in enumerate(POOL_WINDOWS):
        sl = slice(g * POOL_GROUP_DIM, (g + 1) * POOL_GROUP_DIM)
        lo = jnp.clip(t - w // 2, 0, T)
        hi = jnp.clip(t + w // 2, 0, T)
        cnt = (hi - lo).astype(jnp.float32)[:, None]
        diffs.append((cs[:, hi, sl] - cs[:, lo, sl]) / cnt - uf[..., sl])
    d = jnp.stack(diffs, axis=2).astype(u.dtype)
    y = jnp.einsum('btgi,gio->btgo', d, pool_w)
    return y.reshape(u.shape) * pool_scale


def _split_ap(z):
    return jnp.split(z, [ATTN_DIM, ATTN_DIM + KV_DIM, ATTN_DIM + 2 * KV_DIM,
                         ATTN_DIM + 2 * KV_DIM + POOL_DIM], axis=-1)


def _ap_output(attn, u, gz, pool_w, pool_scale, w_out):
    y = jnp.concatenate([attn, _multiscale_pool(u, pool_w, pool_scale)], axis=-1) * jax.nn.silu(gz)
    return y @ w_out


def _attn_pool_layer(x, xc, c, c_ctx, cos, sin, ada_w, ada_b, norm_g, w_in, w_out, sink,
                     pool_w, pool_scale, update_ctx):
    B, L = x.shape[:2]
    h, gate = _modulate(x, c[:, None, :], ada_w, ada_b, norm_g)
    hc, gate_c = _modulate(xc, c_ctx, ada_w, ada_b, norm_g)
    q, k, v, u, gz = _split_ap(h @ w_in)
    q = _rope(_heads(q, N_Q_HEADS), cos, sin).reshape(B, L, N_KV_HEADS, Q_PER_KV, HEAD_DIM)
    k = _rope(_heads(k, N_KV_HEADS), cos, sin)
    v = _heads(v, N_KV_HEADS)
    if update_ctx:
        qc, kc, vc, uc, gzc = _split_ap(hc @ w_in)
    else:
        kc, vc = jnp.split(hc @ w_in[:, ATTN_DIM:ATTN_DIM + 2 * KV_DIM], 2, axis=-1)
    kc, vc = _heads(kc, N_KV_HEADS), _heads(vc, N_KV_HEADS)
    attn = _banded_window_attention(q, k, v, kc, vc, sink)
    x_new = x + gate * _ap_output(attn, u, gz, pool_w, pool_scale, w_out)
    if update_ctx:
        Lc = xc.shape[1]
        qc = _heads(qc, N_Q_HEADS).reshape(B, Lc, N_KV_HEADS, Q_PER_KV, HEAD_DIM)
        attn_c = _sink_attention(qc, kc, vc, sink).reshape(B, Lc, ATTN_DIM)
        xc = xc + gate_c * _ap_output(attn_c, uc, gzc, pool_w, pool_scale, w_out)
    return x_new, xc


def _depthwise_conv(u, w, b):
    y = lax.conv_general_dilated(u, w[:, None, :].astype(u.dtype), window_strides=(1,),
                                 padding=((CONV_K // 2, CONV_K // 2),),
                                 dimension_numbers=('NWC', 'WIO', 'NWC'),
                                 feature_group_count=u.shape[-1])
    return y + b


def _conformer_conv_mixer(z, dw_w, dw_b, ln_g, ln_b, w_out):
    a, b, gz = jnp.split(z, 3, axis=-1)
    u = a * jax.nn.sigmoid(b)
    u = jax.nn.silu(_layernorm(_depthwise_conv(u, dw_w, dw_b), ln_g, ln_b))
    return (u * jax.nn.silu(gz)) @ w_out


def _conv_layer(x, xc, c, c_ctx, ada_w, ada_b, norm_g, w_in, w_out, dw_w, dw_b, ln_g, ln_b, update_ctx):
    h, gate = _modulate(x, c[:, None, :], ada_w, ada_b, norm_g)
    x_new = x + gate * _conformer_conv_mixer(h @ w_in, dw_w, dw_b, ln_g, ln_b, w_out)
    if update_ctx:
        hc, gate_c = _modulate(xc, c_ctx, ada_w, ada_b, norm_g)
        xc = xc + gate_c * _conformer_conv_mixer(hc @ w_in, dw_w, dw_b, ln_g, ln_b, w_out)
    return x_new, xc


def setup_inputs(seed: int = 0) -> dict:
    key = jax.random.key(seed)
    ks = jax.random.split(key, 22)
    D = D_MODEL

    def nrm(k, shape, s):
        return s * jax.random.normal(k, shape, jnp.float32)

    return {
        "x": nrm(ks[0], (BATCH, SEQ, D), 1.0),
        "c": nrm(ks[1], (BATCH, D), 1.0),
        "ctx": nrm(ks[2], (BATCH, CTX_LEN, D), 1.0),
        "c_ctx": nrm(ks[3], (D,), 1.0),
        "ap_ada_w": nrm(ks[4], (N_AP_LAYERS, D, 3 * D), 0.3 * D ** -0.5),
        "ap_ada_b": nrm(ks[5], (N_AP_LAYERS, 3 * D), 0.02),
        "ap_norm_g": 1.0 + nrm(ks[6], (N_AP_LAYERS, D), 0.05),
        "ap_w_in": nrm(ks[7], (N_AP_LAYERS, D, AP_IN_DIM), D ** -0.5),
        "ap_w_out": nrm(ks[8], (N_AP_LAYERS, AP_GATE_DIM, D), AP_GATE_DIM ** -0.5),
        "ap_sink": nrm(ks[9], (N_AP_LAYERS, N_Q_HEADS), 0.5),
        "ap_pool_w": nrm(ks[10], (N_AP_LAYERS, len(POOL_WINDOWS), POOL_GROUP_DIM, POOL_GROUP_DIM),
                          POOL_GROUP_DIM ** -0.5),
        "ap_pool_scale": 1.0 + nrm(ks[11], (N_AP_LAYERS, POOL_DIM), 0.1),
        "cv_ada_w": nrm(ks[12], (N_CV_LAYERS, D, 3 * D), 0.3 * D ** -0.5),
        "cv_ada_b": nrm(ks[13], (N_CV_LAYERS, 3 * D), 0.02),
        "cv_norm_g": 1.0 + nrm(ks[14], (N_CV_LAYERS, D), 0.05),
        "cv_w_in": nrm(ks[15], (N_CV_LAYERS, D, CV_IN_DIM), D ** -0.5),
        "cv_w_out": nrm(ks[16], (N_CV_LAYERS, CONV_DIM, D), CONV_DIM ** -0.5),
        "cv_dw_w": nrm(ks[17], (N_CV_LAYERS, CONV_K, CONV_DIM), CONV_K ** -0.5),
        "cv_dw_b": nrm(ks[18], (N_CV_LAYERS, CONV_DIM), 0.02),
        "cv_ln_g": 1.0 + nrm(ks[19], (N_CV_LAYERS, CONV_DIM), 0.05),
        "cv_ln_b": nrm(ks[20], (N_CV_LAYERS, CONV_DIM), 0.02),
        "final_norm_g": 1.0 + nrm(ks[21], (D,), 0.05),
    }


def reference(x, c, ctx, c_ctx, ap_ada_w, ap_ada_b, ap_norm_g, ap_w_in, ap_w_out, ap_sink,
              ap_pool_w, ap_pool_scale, cv_ada_w, cv_ada_b, cv_norm_g, cv_w_in, cv_w_out,
              cv_dw_w, cv_dw_b, cv_ln_g, cv_ln_b, final_norm_g):
    L = x.shape[1]
    cos, sin = _axial_rope_tables(L)
    last_ap_layer = ((DEPTH - 1) // 2) * 2
    xc = ctx
    for i in range(DEPTH):
        j = i // 2
        update_ctx = i < last_ap_layer
        if i % 2 == 0:
            x, xc = _attn_pool_layer(x, xc, c, c_ctx, cos, sin, ap_ada_w[j], ap_ada_b[j], ap_norm_g[j],
                                     ap_w_in[j], ap_w_out[j], ap_sink[j], ap_pool_w[j],
                                     ap_pool_scale[j], update_ctx)
        else:
            x, xc = _conv_layer(x, xc, c, c_ctx, cv_ada_w[j], cv_ada_b[j], cv_norm_g[j], cv_w_in[j],
                                cv_w_out[j], cv_dw_w[j], cv_dw_b[j], cv_ln_g[j], cv_ln_b[j], update_ctx)
    return _rmsnorm(x, final_norm_g)
```

```python
import functools

import jax
import jax.numpy as jnp
import numpy as np
from jax import lax
from jax.experimental import pallas as pl
from jax.experimental.pallas import tpu as pltpu

F32 = jnp.float32
BF16 = jnp.bfloat16

D_MODEL = 1024
DEPTH = 4
GRID_W = 64
HEAD_DIM = 64
ATTN_DIM = 512
N_Q_HEADS = 8
KV_DIM = 128
WINDOW = 128
ATTN_BLOCK = 128
ROPE_THETA = 10000.0
ROPE_AXIS_PAIRS = HEAD_DIM // 4
POOL_WINDOWS = (2, 4, 8, 16)
POOL_DIM = 512
POOL_GROUP = 128
POOL_HALO = 8
CONV_K = 31
CONV_HALO = 16
EPS = 1e-6
NEG_INF = -1e30

LANES = 128
N_COND_ROWS = 16
VMEM_LIMIT = 56 * 1024 * 1024


def _silu(x):
    return x * jax.nn.sigmoid(x)


def _modnorm(x, g, scale1, shift):
    ms = jnp.mean(x * x, axis=-1, keepdims=True)
    y = x * lax.rsqrt(ms + EPS)
    return (y * g) * scale1 + shift


def _modnorm_rows(dst_ref, dst_off, src_ref, n_rows, chunk, g, scale1, shift):
    def body(j, carry):
        r = pl.multiple_of(j * chunk, chunk)
        x = src_ref[0, pl.ds(r, chunk), :]
        dst_ref[pl.ds(pl.multiple_of(dst_off + r, 16), chunk), :] = (
            _modnorm(x, g, scale1, shift).astype(BF16))
        return carry

    lax.fori_loop(0, n_rows // chunk, body, 0)


def _mod_params(mod_ref):
    shift = mod_ref[0, :, 0:D_MODEL]
    scale1 = 1.0 + mod_ref[0, :, D_MODEL:2 * D_MODEL]
    gate = mod_ref[0, :, 2 * D_MODEL:3 * D_MODEL]
    return shift, scale1, gate


def _rope(x, cos, sin, first_half):
    partner = jnp.where(first_half, pltpu.roll(x, 96, 1), pltpu.roll(x, 32, 1))
    return x * cos + partner * sin


def _lane_masks(rows):
    lane = lax.broadcasted_iota(jnp.int32, (rows, LANES), 1)
    return (lane & 32) == 0, lane < HEAD_DIM


def _softmax_pv(q_stack, segs, sink_col):
    scores = []
    for k, _, mask in segs:
        s = lax.dot_general(q_stack, k, (((1,), (1,)), ((), ())), preferred_element_type=F32)
        if mask is not None:
            s = jnp.concatenate(
                [jnp.where(mask, s[c * ATTN_BLOCK:(c + 1) * ATTN_BLOCK], NEG_INF) for c in range(4)],
                axis=0)
        scores.append(s)
    m = sink_col
    for s in scores:
        m = jnp.maximum(m, jnp.max(s, axis=-1, keepdims=True))
    denom = jnp.exp(sink_col - m)
    out = None
    for s, (_, v, _) in zip(scores, segs):
        p = jnp.exp(s - m)
        denom = denom + jnp.sum(p, axis=-1, keepdims=True)
        pv = jnp.dot(p.astype(BF16), v, preferred_element_type=F32)
        out = pv if out is None else out + pv
    return out / denom


def _sink_col(sink_ref, kv_head):
    return jnp.concatenate(
        [jnp.full((ATTN_BLOCK, 1), sink_ref[c + 4 * kv_head], F32) for c in range(4)], axis=0)


def _pool_stage(u_s, db_s, y_s, pw_ref, ps_ref, t0, n_rows, seq, chunk):
    for rc in range(n_rows // chunk):
        r = rc * chunk
        t = t0 + r + lax.broadcasted_iota(jnp.int32, (chunk, POOL_GROUP), 0)
        for gi, w in enumerate(POOL_WINDOWS):
            sl = slice(gi * POOL_GROUP, (gi + 1) * POOL_GROUP)
            acc = None
            for d in range(-(w // 2), w // 2):
                val = u_s[POOL_HALO + r + d:POOL_HALO + r + d + chunk, sl]
                acc = val if acc is None else acc + val
            cnt = (jnp.minimum(t + w // 2, seq) - jnp.maximum(t - w // 2, 0)).astype(F32)
            diff = acc / cnt - u_s[POOL_HALO + r:POOL_HALO + r + chunk, sl]
            db_s[r:r + chunk, sl] = diff.astype(BF16)
    for gi in range(len(POOL_WINDOWS)):
        sl = slice(gi * POOL_GROUP, (gi + 1) * POOL_GROUP)
        yg = jnp.dot(db_s[:, sl], pw_ref[gi], preferred_element_type=F32) * ps_ref[:, sl]
        y_s[:, ATTN_DIM + gi * POOL_GROUP:ATTN_DIM + (gi + 1) * POOL_GROUP] = yg


def _gate_stage(y_s, gz_s, yb_s, n_rows, chunk):
    def body(j, carry):
        r = pl.multiple_of(j * chunk, chunk)
        rows = pl.ds(r, chunk)
        yb_s[rows, :] = (y_s[rows, :] * _silu(gz_s[rows, :])).astype(BF16)
        return carry

    lax.fori_loop(0, n_rows // chunk, body, 0)


def _ap_kernel(*refs, tq, seq, latent):
    if latent:
        (sink_ref, x_ref, xp_ref, xn_ref, mod_ref, g_ref, wq_ref, wkv_ref, wu_ref, wgz_ref,
         wout_ref, pw_ref, ps_ref, cos_ref, sin_ref, cosp_ref, sinp_ref, cosn_ref, sinn_ref,
         kc_ref, vc_ref, o_ref,
         h_s, zq_s, zkv_s, q_s, kl_s, vl_s, u_s, gz_s, y_s, db_s, yb_s) = refs
        halo = WINDOW
        i = pl.program_id(1)
        n_tiles = seq // tq
    else:
        (sink_ref, x_ref, mod_ref, g_ref, wq_ref, wkv_ref, wu_ref, wgz_ref,
         wout_ref, pw_ref, ps_ref, o_ref, kc_out, vc_out,
         h_s, zq_s, zkv_s, q_s, kl_s, vl_s, u_s, gz_s, y_s, db_s, yb_s) = refs
        halo = 0
        i = 0
        n_tiles = 1
    t0 = i * tq
    n_kv_rows = tq + 2 * halo

    shift, scale1, gate = _mod_params(mod_ref)
    g = g_ref[...]
    if latent:
        _modnorm_rows(h_s, 0, xp_ref, halo, 64, g, scale1, shift)
        _modnorm_rows(h_s, halo + tq, xn_ref, halo, 64, g, scale1, shift)
    _modnorm_rows(h_s, halo, x_ref, tq, 64, g, scale1, shift)

    zkv_s[...] = jnp.dot(h_s[...], wkv_ref[...], preferred_element_type=F32)
    zq_s[...] = jnp.dot(h_s[halo:halo + tq, :], wq_ref[...], preferred_element_type=F32)
    gz_s[...] = jnp.dot(h_s[halo:halo + tq, :], wgz_ref[...], preferred_element_type=F32)
    u_s[POOL_HALO:POOL_HALO + tq, :] = jnp.dot(h_s[halo:halo + tq, :], wu_ref[...],
                                                preferred_element_type=F32)
    zero_halo = jnp.zeros((POOL_HALO, POOL_DIM), F32)
    if latent:
        up = jnp.dot(h_s[halo - 16:halo, :], wu_ref[...], preferred_element_type=F32)
        un = jnp.dot(h_s[halo + tq:halo + tq + 16, :], wu_ref[...], preferred_element_type=F32)
        u_s[0:POOL_HALO, :] = jnp.where(i > 0, up[16 - POOL_HALO:16], zero_halo)
        u_s[POOL_HALO + tq:2 * POOL_HALO + tq, :] = jnp.where(i < n_tiles - 1, un[0:POOL_HALO],
                                                               zero_halo)
    else:
        u_s[0:POOL_HALO, :] = zero_halo
        u_s[POOL_HALO + tq:2 * POOL_HALO + tq, :] = zero_halo

    rc = 64
    first_half, kv0_lanes = _lane_masks(rc)

    def q_body(j, carry):
        r = pl.multiple_of(j * rc, rc)
        rows = pl.ds(r, rc)
        for c in range(4):
            sl = slice(c * LANES, (c + 1) * LANES)
            q = zq_s[rows, sl]
            if latent:
                q = _rope(q, cos_ref[rows, :], sin_ref[rows, :], first_half)
            q_s[rows, sl] = q.astype(BF16)
        return carry

    lax.fori_loop(0, tq // rc, q_body, 0)

    def kv_rows(off, n_rows, c_ref, s_ref):
        def body(j, carry):
            r = pl.multiple_of(j * rc, rc)
            zrows = pl.ds(pl.multiple_of(off + r, 16), rc)
            k = zkv_s[zrows, 0:LANES]
            v = zkv_s[zrows, LANES:2 * LANES]
            if latent:
                k = _rope(k, c_ref[pl.ds(r, rc), :], s_ref[pl.ds(r, rc), :], first_half)
            kl_s[0, zrows, :] = jnp.where(kv0_lanes, k, 0.0).astype(BF16)
            kl_s[1, zrows, :] = jnp.where(kv0_lanes, 0.0, k).astype(BF16)
            vl_s[0, zrows, :] = jnp.where(kv0_lanes, v, 0.0).astype(BF16)
            vl_s[1, zrows, :] = jnp.where(kv0_lanes, 0.0, v).astype(BF16)
            return carry

        lax.fori_loop(0, n_rows // rc, body, 0)

    if latent:
        kv_rows(0, halo, cosp_ref, sinp_ref)
        kv_rows(halo, tq, cos_ref, sin_ref)
        kv_rows(halo + tq, halo, cosn_ref, sinn_ref)
    else:
        kv_rows(0, tq, None, None)
        kc_out[0] = kl_s[...]
        vc_out[0] = vl_s[...]

    def attn_body(j, carry):
        r = pl.multiple_of(j * ATTN_BLOCK, ATTN_BLOCK)
        rows = pl.ds(r, ATTN_BLOCK)
        q_stack = jnp.concatenate([q_s[rows, c * LANES:(c + 1) * LANES] for c in range(4)], axis=0)
        if latent:
            q0 = t0 + r
            rr = lax.broadcasted_iota(jnp.int32, (ATTN_BLOCK, 3 * ATTN_BLOCK), 0)
            cc = lax.broadcasted_iota(jnp.int32, (ATTN_BLOCK, 3 * ATTN_BLOCK), 1)
            lo = jnp.maximum(rr, ATTN_BLOCK - q0)
            hi = jnp.minimum(rr + 2 * WINDOW, seq - 1 + ATTN_BLOCK - q0)
            mask = (cc >= lo) & (cc <= hi)
        acc = None
        for kvh in range(2):
            if latent:
                segs = [(kl_s[kvh, pl.ds(r, 3 * ATTN_BLOCK), :], vl_s[kvh, pl.ds(r, 3 * ATTN_BLOCK), :],
                         mask),
                        (kc_ref[0, kvh], vc_ref[0, kvh], None)]
            else:
                segs = [(kl_s[kvh], vl_s[kvh], None)]
            o = _softmax_pv(q_stack, segs, _sink_col(sink_ref, kvh))
            acc = o if acc is None else acc + o
        for c in range(4):
            y_s[rows, c * LANES:(c + 1) * LANES] = acc[c * ATTN_BLOCK:(c + 1) * ATTN_BLOCK]
        return carry

    lax.fori_loop(0, tq // ATTN_BLOCK, attn_body, 0)

    _pool_stage(u_s, db_s, y_s, pw_ref, ps_ref, t0, tq, seq, 64)
    _gate_stage(y_s, gz_s, yb_s, tq, 64)
    out = jnp.dot(yb_s[...], wout_ref[...], preferred_element_type=F32)
    o_ref[0] = x_ref[0] + gate * out


def _ctx_kv_kernel(x_ref, mod_ref, g_ref, wkv_ref, kc_out, vc_out, h_s, *, n_rows):
    shift, scale1, _ = _mod_params(mod_ref)
    _modnorm_rows(h_s, 0, x_ref, n_rows, 64, g_ref[...], scale1, shift)
    z = jnp.dot(h_s[...], wkv_ref[...], preferred_element_type=F32)
    _, kv0_lanes = _lane_masks(n_rows)
    k = z[:, 0:LANES]
    v = z[:, LANES:2 * LANES]
    kc_out[0, 0] = jnp.where(kv0_lanes, k, 0.0).astype(BF16)
    kc_out[0, 1] = jnp.where(kv0_lanes, 0.0, k).astype(BF16)
    vc_out[0, 0] = jnp.where(kv0_lanes, v, 0.0).astype(BF16)
    vc_out[0, 1] = jnp.where(kv0_lanes, 0.0, v).astype(BF16)


def _cv_kernel(*refs, tq, seq, has_halo, final_norm):
    refs = list(refs)
    x_ref = refs.pop(0)
    if has_halo:
        xp_ref = refs.pop(0)
        xn_ref = refs.pop(0)
    (mod_ref, g_ref, wab_ref, wgz_ref, wout_ref, dww_ref, dwb_ref, lng_ref, lnb_ref) = refs[:9]
    refs = refs[9:]
    if final_norm:
        fg_ref = refs.pop(0)
    o_ref, h_s, zab_s, u_s, gz_s, c_s, yb_s = refs
    halo = CONV_HALO
    i = pl.program_id(1) if has_halo else 0
    t0 = i * tq

    shift, scale1, gate = _mod_params(mod_ref)
    g = g_ref[...]
    _modnorm_rows(h_s, halo, x_ref, tq, 64, g, scale1, shift)
    if has_halo:
        _modnorm_rows(h_s, 0, xp_ref, halo, halo, g, scale1, shift)
        _modnorm_rows(h_s, halo + tq, xn_ref, halo, halo, g, scale1, shift)
        zab_s[...] = jnp.dot(h_s[...], wab_ref[...], preferred_element_type=F32)
        z_rows, z_off = tq + 2 * halo, 0
    else:
        zab_s[halo:halo + tq, :] = jnp.dot(h_s[halo:halo + tq, :], wab_ref[...],
                                           preferred_element_type=F32)
        zero_halo = jnp.zeros((halo, D_MODEL), F32)
        u_s[0:halo, :] = zero_halo
        u_s[halo + tq:2 * halo + tq, :] = zero_halo
        z_rows, z_off = tq, halo
    gz_s[...] = jnp.dot(h_s[halo:halo + tq, :], wgz_ref[...], preferred_element_type=F32)

    def glu_body(j, carry):
        r = pl.multiple_of(z_off + j * 16, 16)
        rows = pl.ds(r, 16)
        u = zab_s[rows, 0:D_MODEL] * jax.nn.sigmoid(zab_s[rows, D_MODEL:2 * D_MODEL])
        pos = t0 - halo + r + lax.broadcasted_iota(jnp.int32, (16, D_MODEL), 0)
        u_s[rows, :] = jnp.where((pos >= 0) & (pos < seq), u, 0.0)
        return carry

    lax.fori_loop(0, z_rows // 16, glu_body, 0)

    cr = 64

    def conv_body(j, carry):
        r = pl.multiple_of(j * cr, cr)
        for c in range(D_MODEL // LANES):
            sl = slice(c * LANES, (c + 1) * LANES)
            win = u_s[pl.ds(r, cr + 2 * halo), sl]
            acc = None
            for k in range(CONV_K):
                term = win[k + 1:k + 1 + cr] * dww_ref[k:k + 1, sl]
                acc = term if acc is None else acc + term
            c_s[pl.ds(r, cr), sl] = acc + dwb_ref[:, sl]
        return carry

    lax.fori_loop(0, tq // cr, conv_body, 0)

    lng = lng_ref[...]
    lnb = lnb_ref[...]

    def ln_body(j, carry):
        r = pl.multiple_of(j * 32, 32)
        rows = pl.ds(r, 32)
        y = c_s[rows, :]
        mu = jnp.mean(y, axis=-1, keepdims=True)
        yc = y - mu
        yn = yc * lax.rsqrt(jnp.mean(yc * yc, axis=-1, keepdims=True) + EPS)
        u2 = _silu(yn * lng + lnb)
        yb_s[rows, :] = (u2 * _silu(gz_s[rows, :])).astype(BF16)
        return carry

    lax.fori_loop(0, tq // 32, ln_body, 0)

    out = jnp.dot(yb_s[...], wout_ref[...], preferred_element_type=F32)
    x_new = x_ref[0] + gate * out
    if final_norm:
        ms = jnp.mean(x_new * x_new, axis=-1, keepdims=True)
        x_new = (x_new * lax.rsqrt(ms + EPS)) * fg_ref[...]
    o_ref[0] = x_new


def _ada_kernel(c_ref, w_ref, b_ref, o_ref):
    s = _silu(c_ref[...])
    o_ref[0] = jnp.dot(s, w_ref[0], precision=lax.Precision.HIGHEST,
                       preferred_element_type=F32) + b_ref[0]


def _ada_all(cond, ada_w, ada_b):
    n_layers = ada_w.shape[0]
    tn = 768
    return pl.pallas_call(
        _ada_kernel,
        grid=(n_layers, 3 * D_MODEL // tn),
        in_specs=[pl.BlockSpec((N_COND_ROWS, D_MODEL), lambda l, n: (0, 0)),
                  pl.BlockSpec((1, D_MODEL, tn), lambda l, n: (l, 0, n)),
                  pl.BlockSpec((1, 1, tn), lambda l, n: (l, 0, n))],
        out_specs=pl.BlockSpec((1, N_COND_ROWS, tn), lambda l, n: (l, 0, n)),
        out_shape=jax.ShapeDtypeStruct((n_layers, N_COND_ROWS, 3 * D_MODEL), F32),
        compiler_params=pltpu.CompilerParams(dimension_semantics=("arbitrary", "arbitrary"),
                                             vmem_limit_bytes=VMEM_LIMIT),
        name="adaln",
    )(cond, ada_w, ada_b.reshape(n_layers, 1, 3 * D_MODEL))


def _full(shape):
    return pl.BlockSpec(shape, lambda *_: (0,) * len(shape))


def _ap_scratch(tq, halo):
    n_kv = tq + 2 * halo
    return [pltpu.VMEM((n_kv, D_MODEL), BF16),
            pltpu.VMEM((tq, ATTN_DIM), F32),
            pltpu.VMEM((n_kv, 2 * KV_DIM), F32),
            pltpu.VMEM((tq, ATTN_DIM), BF16),
            pltpu.VMEM((2, n_kv, KV_DIM), BF16),
            pltpu.VMEM((2, n_kv, KV_DIM), BF16),
            pltpu.VMEM((tq + 2 * POOL_HALO, POOL_DIM), F32),
            pltpu.VMEM((tq, D_MODEL), F32),
            pltpu.VMEM((tq, D_MODEL), F32),
            pltpu.VMEM((tq, POOL_DIM), BF16),
            pltpu.VMEM((tq, D_MODEL), BF16)]


def _ap_latent(x, mod, w, sink, rope, kc, vc, tq):
    batch, seq, _ = x.shape
    bpt = tq // WINDOW
    n_blk = seq // WINDOW
    main = lambda b, i: (b, i, 0)
    prev = lambda b, i: (b, jnp.maximum(i * bpt - 1, 0), 0)
    nxt = lambda b, i: (b, jnp.minimum((i + 1) * bpt, n_blk - 1), 0)
    tmain = lambda b, i: (i, 0)
    tprev = lambda b, i: (jnp.maximum(i * bpt - 1, 0), 0)
    tnxt = lambda b, i: (jnp.minimum((i + 1) * bpt, n_blk - 1), 0)
    cos, sin = rope
    in_specs = [
        pl.BlockSpec(memory_space=pltpu.SMEM),
        pl.BlockSpec((1, tq, D_MODEL), main),
        pl.BlockSpec((1, WINDOW, D_MODEL), prev),
        pl.BlockSpec((1, WINDOW, D_MODEL), nxt),
        pl.BlockSpec((1, 1, 3 * D_MODEL), lambda b, i: (b, 0, 0)),
        _full((1, D_MODEL)),
        _full((D_MODEL, ATTN_DIM)), _full((D_MODEL, 2 * KV_DIM)), _full((D_MODEL, POOL_DIM)),
        _full((D_MODEL, D_MODEL)), _full((D_MODEL, D_MODEL)),
        _full((len(POOL_WINDOWS), POOL_GROUP, POOL_GROUP)), _full((1, POOL_DIM)),
        pl.BlockSpec((tq, LANES), tmain), pl.BlockSpec((tq, LANES), tmain),
        pl.BlockSpec((WINDOW, LANES), tprev), pl.BlockSpec((WINDOW, LANES), tprev),
        pl.BlockSpec((WINDOW, LANES), tnxt), pl.BlockSpec((WINDOW, LANES), tnxt),
        pl.BlockSpec((1, 2, kc.shape[2], KV_DIM), lambda b, i: (b, 0, 0, 0)),
        pl.BlockSpec((1, 2, vc.shape[2], KV_DIM), lambda b, i: (b, 0, 0, 0)),
    ]
    return pl.pallas_call(
        functools.partial(_ap_kernel, tq=tq, seq=seq, latent=True),
        grid=(batch, seq // tq),
        in_specs=in_specs,
        out_specs=pl.BlockSpec((1, tq, D_MODEL), main),
        out_shape=jax.ShapeDtypeStruct(x.shape, F32),
        scratch_shapes=_ap_scratch(tq, WINDOW),
        compiler_params=pltpu.CompilerParams(dimension_semantics=("arbitrary", "arbitrary"),
                                             vmem_limit_bytes=VMEM_LIMIT),
        name="ap_latent",
    )(sink, x, x, x, mod, w["norm_g"], w["wq"], w["wkv"], w["wu"], w["wgz"], w["wout"],
      w["pool_w"], w["pool_scale"], cos, sin, cos, sin, cos, sin, kc, vc)


def _ap_ctx(xc, mod_ctx, w, sink):
    batch, n, _ = xc.shape
    row = lambda b: (b, 0, 0)
    kv_spec = pl.BlockSpec((1, 2, n, KV_DIM), lambda b: (b, 0, 0, 0))
    in_specs = [
        pl.BlockSpec(memory_space=pltpu.SMEM),
        pl.BlockSpec((1, n, D_MODEL), row),
        _full((1, 1, 3 * D_MODEL)),
        _full((1, D_MODEL)),
        _full((D_MODEL, ATTN_DIM)), _full((D_MODEL, 2 * KV_DIM)), _full((D_MODEL, POOL_DIM)),
        _full((D_MODEL, D_MODEL)), _full((D_MODEL, D_MODEL)),
        _full((len(POOL_WINDOWS), POOL_GROUP, POOL_GROUP)), _full((1, POOL_DIM)),
    ]
    kv_shape = jax.ShapeDtypeStruct((batch, 2, n, KV_DIM), BF16)
    return pl.pallas_call(
        functools.partial(_ap_kernel, tq=n, seq=n, latent=False),
        grid=(batch,),
        in_specs=in_specs,
        out_specs=[pl.BlockSpec((1, n, D_MODEL), row), kv_spec, kv_spec],
        out_shape=[jax.ShapeDtypeStruct(xc.shape, F32), kv_shape, kv_shape],
        scratch_shapes=_ap_scratch(n, 0),
        compiler_params=pltpu.CompilerParams(dimension_semantics=("arbitrary",),
                                             vmem_limit_bytes=VMEM_LIMIT),
        name="ap_ctx",
    )(sink, xc, mod_ctx, w["norm_g"], w["wq"], w["wkv"], w["wu"], w["wgz"], w["wout"],
      w["pool_w"], w["pool_scale"])


def _ctx_kv(xc, mod_ctx, w):
    batch, n, _ = xc.shape
    kv_spec = pl.BlockSpec((1, 2, n, KV_DIM), lambda b: (b, 0, 0, 0))
    kv_shape = jax.ShapeDtypeStruct((batch, 2, n, KV_DIM), BF16)
    return pl.pallas_call(
        functools.partial(_ctx_kv_kernel, n_rows=n),
        grid=(batch,),
        in_specs=[pl.BlockSpec((1, n, D_MODEL), lambda b: (b, 0, 0)),
                  _full((1, 1, 3 * D_MODEL)), _full((1, D_MODEL)), _full((D_MODEL, 2 * KV_DIM))],
        out_specs=[kv_spec, kv_spec],
        out_shape=[kv_shape, kv_shape],
        scratch_shapes=[pltpu.VMEM((n, D_MODEL), BF16)],
        compiler_params=pltpu.CompilerParams(dimension_semantics=("arbitrary",),
                                             vmem_limit_bytes=VMEM_LIMIT),
        name="ctx_kv",
    )(xc, mod_ctx, w["norm_g"], w["wkv"])


def _cv_layer(x, mod, w, tq, has_halo, final_g=None):
    batch, seq, _ = x.shape
    halo = CONV_HALO
    final_norm = final_g is not None
    if has_halo:
        bpt = tq // halo
        n_blk = seq // halo
        grid = (batch, seq // tq)
        main = lambda b, i: (b, i, 0)
        x_specs = [pl.BlockSpec((1, tq, D_MODEL), main),
                   pl.BlockSpec((1, halo, D_MODEL), lambda b, i: (b, jnp.maximum(i * bpt - 1, 0), 0)),
                   pl.BlockSpec((1, halo, D_MODEL),
                                lambda b, i: (b, jnp.minimum((i + 1) * bpt, n_blk - 1), 0))]
        x_args = [x, x, x]
        mod_spec = pl.BlockSpec((1, 1, 3 * D_MODEL), lambda b, i: (b, 0, 0))
        sem = ("arbitrary", "arbitrary")
    else:
        assert tq == seq
        grid = (batch,)
        main = lambda b: (b, 0, 0)
        x_specs = [pl.BlockSpec((1, tq, D_MODEL), main)]
        x_args = [x]
        mod_spec = _full((1, 1, 3 * D_MODEL))
        sem = ("arbitrary",)
    in_specs = x_specs + [
        mod_spec, _full((1, D_MODEL)),
        _full((D_MODEL, 2 * D_MODEL)), _full((D_MODEL, D_MODEL)), _full((D_MODEL, D_MODEL)),
        _full((CONV_K, D_MODEL)), _full((1, D_MODEL)), _full((1, D_MODEL)), _full((1, D_MODEL)),
    ]
    args = x_args + [mod, w["norm_g"], w["wab"], w["wgz"], w["wout"], w["dw_w"], w["dw_b"],
                     w["ln_g"], w["ln_b"]]
    if final_norm:
        in_specs.append(_full((1, D_MODEL)))
        args.append(final_g)
    scratch = [pltpu.VMEM((tq + 2 * halo, D_MODEL), BF16),
               pltpu.VMEM((tq + 2 * halo, 2 * D_MODEL), F32),
               pltpu.VMEM((tq + 2 * halo, D_MODEL), F32),
               pltpu.VMEM((tq, D_MODEL), F32),
               pltpu.VMEM((tq, D_MODEL), F32),
               pltpu.VMEM((tq, D_MODEL), BF16)]
    return pl.pallas_call(
        functools.partial(_cv_kernel, tq=tq, seq=seq, has_halo=has_halo, final_norm=final_norm),
        grid=grid,
        in_specs=in_specs,
        out_specs=pl.BlockSpec((1, tq, D_MODEL), main),
        out_shape=jax.ShapeDtypeStruct(x.shape, F32),
        scratch_shapes=scratch,
        compiler_params=pltpu.CompilerParams(dimension_semantics=sem,
                                             vmem_limit_bytes=VMEM_LIMIT),
        name="cv_latent" if has_halo else "cv_ctx",
    )(*args)


def _rope_tables(seq):
    rows = seq // GRID_W
    row = jnp.repeat(jnp.arange(rows, dtype=F32), GRID_W)
    col = jnp.tile(jnp.arange(GRID_W, dtype=F32), rows)
    inv_freq = ROPE_THETA ** (-jnp.arange(ROPE_AXIS_PAIRS, dtype=F32) / ROPE_AXIS_PAIRS)
    ang = jnp.concatenate([row[:, None] * inv_freq, col[:, None] * inv_freq], axis=-1)
    cos, sin = jnp.cos(ang), jnp.sin(ang)
    cos_slab = jnp.tile(cos, (1, 4))
    sin_slab = jnp.tile(jnp.concatenate([-sin, sin], axis=-1), (1, 2))
    return cos_slab, sin_slab


def _head_perm():
    perm = []
    for c in range(4):
        for half in range(2):
            head = c + 4 * half
            perm.extend(range(head * HEAD_DIM, (head + 1) * HEAD_DIM))
    return np.asarray(perm, dtype=np.int32)


def _ap_weights(j, ap_norm_g, ap_w_in, ap_w_out, ap_pool_w, ap_pool_scale):
    perm = _head_perm()
    w_in = ap_w_in[j]
    o_k = ATTN_DIM
    o_u = ATTN_DIM + 2 * KV_DIM
    o_gz = o_u + POOL_DIM
    wq = w_in[:, :ATTN_DIM][:, perm] * (HEAD_DIM ** -0.5)
    wgz = w_in[:, o_gz:]
    wgz = jnp.concatenate([wgz[:, :ATTN_DIM][:, perm], wgz[:, ATTN_DIM:]], axis=1)
    w_out = ap_w_out[j]
    w_out = jnp.concatenate([w_out[:ATTN_DIM][perm], w_out[ATTN_DIM:]], axis=0)
    return {
        "norm_g": ap_norm_g[j].reshape(1, D_MODEL),
        "wq": wq.astype(BF16),
        "wkv": w_in[:, o_k:o_u].astype(BF16),
        "wu": w_in[:, o_u:o_gz].astype(BF16),
        "wgz": wgz.astype(BF16),
        "wout": w_out.astype(BF16),
        "pool_w": ap_pool_w[j].astype(BF16),
        "pool_scale": ap_pool_scale[j].reshape(1, POOL_DIM),
    }


def _cv_weights(j, cv_norm_g, cv_w_in, cv_w_out, cv_dw_w, cv_dw_b, cv_ln_g, cv_ln_b):
    w_in = cv_w_in[j]
    return {
        "norm_g": cv_norm_g[j].reshape(1, D_MODEL),
        "wab": w_in[:, :2 * D_MODEL].astype(BF16),
        "wgz": w_in[:, 2 * D_MODEL:].astype(BF16),
        "wout": cv_w_out[j].astype(BF16),
        "dw_w": cv_dw_w[j],
        "dw_b": cv_dw_b[j].reshape(1, D_MODEL),
        "ln_g": cv_ln_g[j].reshape(1, D_MODEL),
        "ln_b": cv_ln_b[j].reshape(1, D_MODEL),
    }


def kernel(x, c, ctx, c_ctx, ap_ada_w, ap_ada_b, ap_norm_g, ap_w_in, ap_w_out, ap_sink, ap_pool_w, ap_pool_scale, cv_ada_w, cv_ada_b, cv_norm_g, cv_w_in, cv_w_out, cv_dw_w, cv_dw_b, cv_ln_g, cv_ln_b, final_norm_g):
    batch, seq, _ = x.shape
    tq = 512
    cond = jnp.zeros((N_COND_ROWS, D_MODEL), F32).at[:batch].set(c).at[batch].set(c_ctx)
    ap_mod = _ada_all(cond, ap_ada_w, ap_ada_b).reshape(-1, N_COND_ROWS, 1, 3 * D_MODEL)
    cv_mod = _ada_all(cond, cv_ada_w, cv_ada_b).reshape(-1, N_COND_ROWS, 1, 3 * D_MODEL)
    rope = _rope_tables(seq)
    last_ap_layer = ((DEPTH - 1) // 2) * 2
    xc = ctx
    for i in range(DEPTH):
        j = i // 2
        update_ctx = i < last_ap_layer
        if i % 2 == 0:
            w = _ap_weights(j, ap_norm_g, ap_w_in, ap_w_out, ap_pool_w, ap_pool_scale)
            mod = ap_mod[j]
            mod_ctx = mod[batch:batch + 1]
            if update_ctx:
                xc, kc, vc = _ap_ctx(xc, mod_ctx, w, ap_sink[j])
            else:
                kc, vc = _ctx_kv(xc, mod_ctx, w)
            x = _ap_latent(x, mod, w, ap_sink[j], rope, kc, vc, tq)
        else:
            w = _cv_weights(j, cv_norm_g, cv_w_in, cv_w_out, cv_dw_w, cv_dw_b, cv_ln_g, cv_ln_b)
            mod = cv_mod[j]
            if update_ctx:
                xc = _cv_layer(xc, mod[batch:batch + 1], w, xc.shape[1], has_halo=False)
            final_g = final_norm_g.reshape(1, D_MODEL) if i == DEPTH - 1 else None
            x = _cv_layer(x, mod, w, tq, has_halo=True, final_g=final_g)
    return x
```

```python
import functools

import jax
import jax.numpy as jnp
import numpy as np
from jax import lax
from jax.experimental import pallas as pl
from jax.experimental.pallas import tpu as pltpu

F32 = jnp.float32
BF16 = jnp.bfloat16

D_MODEL = 1024
DEPTH = 4
GRID_W = 64
HEAD_DIM = 64
ATTN_DIM = 512
N_Q_HEADS = 8
KV_DIM = 128
WINDOW = 128
ATTN_BLOCK = 128
ROPE_THETA = 10000.0
ROPE_AXIS_PAIRS = HEAD_DIM // 4
POOL_WINDOWS = (2, 4, 8, 16)
POOL_DIM = 512
POOL_GROUP = 128
POOL_HALO = 8
CONV_K = 31
CONV_HALO = 16
EPS = 1e-6
NEG_INF = -1e30

LANES = 128
SUBLANES = 8
N_COND_ROWS = 16
VMEM_LIMIT = 56 * 1024 * 1024
NT_DIMS = (((1,), (1,)), ((), ()))


def _sigmoid(x):
    return 0.5 * jnp.tanh(0.5 * x) + 0.5


def _silu(x):
    return x * _sigmoid(x)


def _modnorm(x, g, scale1, shift):
    ms = jnp.mean(x * x, axis=-1, keepdims=True)
    y = x * lax.rsqrt(ms + EPS)
    return (y * g) * scale1 + shift


def _modnorm_rows(dst_ref, dst_off, src_ref, n_rows, chunk, g, scale1, shift):
    def body(j, carry):
        r = pl.multiple_of(j * chunk, chunk)
        x = src_ref[0, pl.ds(r, chunk), :]
        dst_ref[pl.ds(pl.multiple_of(dst_off + r, 16), chunk), :] = (
            _modnorm(x, g, scale1, shift).astype(BF16))
        return carry

    lax.fori_loop(0, n_rows // chunk, body, 0)


def _mod_params(mod_ref):
    shift = mod_ref[0, :, 0:D_MODEL]
    scale1 = 1.0 + mod_ref[0, :, D_MODEL:2 * D_MODEL]
    gate = mod_ref[0, :, 2 * D_MODEL:3 * D_MODEL]
    return shift, scale1, gate


def _rope(x, cos, sin):
    return x * cos + pltpu.roll(x, 64, 1) * sin


def _head_a_lanes(rows):
    lane = lax.broadcasted_iota(jnp.int32, (rows, LANES), 1)
    return (lane & 32) == 0


def _attention(q_stack, k_blocks, vt_blocks, masks, sink_row):
    scores = []
    for k, mask in zip(k_blocks, masks):
        s = lax.dot_general(k, q_stack, NT_DIMS, preferred_element_type=F32)
        if mask is not None:
            s = jnp.where(mask, s, NEG_INF)
        scores.append(s)
    m = sink_row
    for s in scores:
        m = jnp.maximum(m, jnp.max(s, axis=0, keepdims=True))
    denom = jnp.exp(sink_row - m)
    probs = []
    for s in scores:
        p = jnp.exp(s - m)
        denom = denom + jnp.sum(p, axis=0, keepdims=True)
        probs.append(p.astype(BF16))
    out = jnp.dot(jnp.concatenate(vt_blocks, axis=1), jnp.concatenate(probs, axis=0),
                  preferred_element_type=F32)
    return out / denom


def _sink_row(sink_ref, kv_head):
    return jnp.concatenate(
        [jnp.full((1, LANES), sink_ref[c + 4 * kv_head], F32) for c in range(4)], axis=1)


def _pool_stage(u_s, db_s, y_s, pw_ref, ps_ref, t0, n_rows, seq, chunk):
    for rc in range(n_rows // chunk):
        r = rc * chunk
        t = t0 + r + lax.broadcasted_iota(jnp.int32, (chunk, POOL_GROUP), 0)
        for gi, w in enumerate(POOL_WINDOWS):
            sl = slice(gi * POOL_GROUP, (gi + 1) * POOL_GROUP)
            acc = None
            for d in range(-(w // 2), w // 2):
                val = u_s[POOL_HALO + r + d:POOL_HALO + r + d + chunk, sl]
                acc = val if acc is None else acc + val
            cnt = (jnp.minimum(t + w // 2, seq) - jnp.maximum(t - w // 2, 0)).astype(F32)
            diff = acc / cnt - u_s[POOL_HALO + r:POOL_HALO + r + chunk, sl]
            db_s[r:r + chunk, sl] = diff.astype(BF16)
    for gi in range(len(POOL_WINDOWS)):
        sl = slice(gi * POOL_GROUP, (gi + 1) * POOL_GROUP)
        yg = jnp.dot(db_s[:, sl], pw_ref[gi], preferred_element_type=F32) * ps_ref[:, sl]
        y_s[:, ATTN_DIM + gi * POOL_GROUP:ATTN_DIM + (gi + 1) * POOL_GROUP] = yg


def _gate_stage(y_s, gz_s, yb_s, n_rows, chunk):
    def body(j, carry):
        r = pl.multiple_of(j * chunk, chunk)
        rows = pl.ds(r, chunk)
        yb_s[rows, :] = (y_s[rows, :] * _silu(gz_s[rows, :])).astype(BF16)
        return carry

    lax.fori_loop(0, n_rows // chunk, body, 0)


def _ap_kernel(*refs, tq, seq, latent):
    if latent:
        (sink_ref, x_ref, xp_ref, xn_ref, mod_ref, g_ref, wq_ref, wkv_ref, wu_ref, wgz_ref,
         wout_ref, pw_ref, ps_ref, cos_ref, sin_ref, cosp_ref, sinp_ref, cosn_ref, sinn_ref,
         kc_ref, vct_ref, o_ref,
         h_s, zq_s, zkv_s, q_s, kl_s, vt_s, u_s, gz_s, y_s, db_s, yb_s) = refs
        halo = WINDOW
        i = pl.program_id(1)
        n_tiles = seq // tq
    else:
        (sink_ref, x_ref, mod_ref, g_ref, wq_ref, wkv_ref, wu_ref, wgz_ref,
         wout_ref, pw_ref, ps_ref, o_ref, kc_out, vct_out,
         h_s, zq_s, zkv_s, q_s, kl_s, vt_s, u_s, gz_s, y_s, db_s, yb_s) = refs
        halo = 0
        i = 0
        n_tiles = 1
    t0 = i * tq
    n_blocks = tq // ATTN_BLOCK
    halo_blocks = halo // ATTN_BLOCK

    shift, scale1, gate = _mod_params(mod_ref)
    g = g_ref[...]
    if latent:
        _modnorm_rows(h_s, 0, xp_ref, halo, 128, g, scale1, shift)
        _modnorm_rows(h_s, halo + tq, xn_ref, halo, 128, g, scale1, shift)
    _modnorm_rows(h_s, halo, x_ref, tq, 128, g, scale1, shift)

    zkv_s[...] = jnp.dot(h_s[...], wkv_ref[...], preferred_element_type=F32)
    zq_s[...] = jnp.dot(h_s[halo:halo + tq, :], wq_ref[...], preferred_element_type=F32)
    gz_s[...] = jnp.dot(h_s[halo:halo + tq, :], wgz_ref[...], preferred_element_type=F32)
    u_s[POOL_HALO:POOL_HALO + tq, :] = jnp.dot(h_s[halo:halo + tq, :], wu_ref[...],
                                                preferred_element_type=F32)
    zero_halo = jnp.zeros((POOL_HALO, POOL_DIM), F32)
    if latent:
        up = jnp.dot(h_s[halo - 16:halo, :], wu_ref[...], preferred_element_type=F32)
        un = jnp.dot(h_s[halo + tq:halo + tq + 16, :], wu_ref[...], preferred_element_type=F32)
        u_s[0:POOL_HALO, :] = jnp.where(i > 0, up[16 - POOL_HALO:16], zero_halo)
        u_s[POOL_HALO + tq:2 * POOL_HALO + tq, :] = jnp.where(i < n_tiles - 1, un[0:POOL_HALO],
                                                               zero_halo)
    else:
        u_s[0:POOL_HALO, :] = zero_halo
        u_s[POOL_HALO + tq:2 * POOL_HALO + tq, :] = zero_halo

    rc = ATTN_BLOCK
    head_a = _head_a_lanes(rc)
    dim_a = lax.broadcasted_iota(jnp.int32, (KV_DIM, ATTN_BLOCK), 0) < HEAD_DIM

    def q_body(j, carry):
        r = pl.multiple_of(j * rc, rc)
        rows = pl.ds(r, rc)
        for c in range(4):
            sl = slice(c * LANES, (c + 1) * LANES)
            q = zq_s[rows, sl]
            if latent:
                q = _rope(q, cos_ref[rows, :], sin_ref[rows, :])
            q_s[rows, sl] = q.astype(BF16)
        return carry

    lax.fori_loop(0, tq // rc, q_body, 0)

    def kv_rows(blk_off, n_rows, c_ref, s_ref):
        def body(j, carry):
            r = pl.multiple_of(j * rc, rc)
            zrows = pl.ds(pl.multiple_of(blk_off * rc + r, rc), rc)
            k = zkv_s[zrows, 0:LANES]
            vt = zkv_s[zrows, LANES:2 * LANES].T
            if latent:
                k = _rope(k, c_ref[pl.ds(r, rc), :], s_ref[pl.ds(r, rc), :])
            kl_s[0, zrows, :] = jnp.where(head_a, k, 0.0).astype(BF16)
            kl_s[1, zrows, :] = jnp.where(head_a, 0.0, k).astype(BF16)
            vt_s[0, blk_off + j] = jnp.where(dim_a, vt, 0.0).astype(BF16)
            vt_s[1, blk_off + j] = jnp.where(dim_a, 0.0, vt).astype(BF16)
            return carry

        lax.fori_loop(0, n_rows // rc, body, 0)

    if latent:
        kv_rows(0, halo, cosp_ref, sinp_ref)
        kv_rows(halo_blocks, tq, cos_ref, sin_ref)
        kv_rows(halo_blocks + n_blocks, halo, cosn_ref, sinn_ref)
    else:
        kv_rows(0, tq, None, None)
        kc_out[0] = kl_s[...]
        for kvh in range(2):
            for b in range(n_blocks):
                vct_out[0, kvh, :, b * ATTN_BLOCK:(b + 1) * ATTN_BLOCK] = vt_s[kvh, b]

    if latent:
        kk = lax.broadcasted_iota(jnp.int32, (ATTN_BLOCK, 4 * ATTN_BLOCK), 0)
        qq = lax.broadcasted_iota(jnp.int32, (ATTN_BLOCK, 4 * ATTN_BLOCK), 1) & (ATTN_BLOCK - 1)
        no_prev = jnp.where(i > 0, 0, ATTN_BLOCK)
        no_next = jnp.where(i < n_tiles - 1, 0, ATTN_BLOCK)
    for j in range(n_blocks):
        r = j * ATTN_BLOCK
        q_stack = jnp.concatenate([q_s[r:r + ATTN_BLOCK, c * LANES:(c + 1) * LANES] for c in range(4)],
                                  axis=0)
        if latent:
            prev_ok = kk >= (qq + no_prev if j == 0 else qq)
            next_ok = kk <= (qq - no_next if j == n_blocks - 1 else qq)
            masks = [prev_ok, None, next_ok, None]
        acc = None
        for kvh in range(2):
            if latent:
                k_blocks = [kl_s[kvh, r + b * ATTN_BLOCK:r + (b + 1) * ATTN_BLOCK, :] for b in range(3)]
                k_blocks.append(kc_ref[0, kvh])
                vt_blocks = [vt_s[kvh, j + b] for b in range(3)] + [vct_ref[0, kvh]]
            else:
                k_blocks = [kl_s[kvh]]
                vt_blocks = [jnp.concatenate([vt_s[kvh, b] for b in range(n_blocks)], axis=1)]
                masks = [None]
            o = _attention(q_stack, k_blocks, vt_blocks, masks, _sink_row(sink_ref, kvh))
            acc = o if acc is None else acc + o
        for c in range(4):
            y_s[r:r + ATTN_BLOCK, c * LANES:(c + 1) * LANES] = acc[:, c * ATTN_BLOCK:(c + 1) * ATTN_BLOCK].T

    _pool_stage(u_s, db_s, y_s, pw_ref, ps_ref, t0, tq, seq, 64)
    _gate_stage(y_s, gz_s, yb_s, tq, 64)
    out = jnp.dot(yb_s[...], wout_ref[...], preferred_element_type=F32)
    o_ref[0] = x_ref[0] + gate * out


def _ctx_kv_kernel(x_ref, mod_ref, g_ref, wkv_ref, kc_out, vct_out, h_s, *, n_rows):
    shift, scale1, _ = _mod_params(mod_ref)
    _modnorm_rows(h_s, 0, x_ref, n_rows, 128, g_ref[...], scale1, shift)
    z = jnp.dot(h_s[...], wkv_ref[...], preferred_element_type=F32)
    head_a = _head_a_lanes(n_rows)
    k = z[:, 0:LANES]
    kc_out[0, 0] = jnp.where(head_a, k, 0.0).astype(BF16)
    kc_out[0, 1] = jnp.where(head_a, 0.0, k).astype(BF16)
    dim_a = lax.broadcasted_iota(jnp.int32, (KV_DIM, ATTN_BLOCK), 0) < HEAD_DIM
    for b in range(n_rows // ATTN_BLOCK):
        cols = slice(b * ATTN_BLOCK, (b + 1) * ATTN_BLOCK)
        vt = z[cols, LANES:2 * LANES].T
        vct_out[0, 0, :, cols] = jnp.where(dim_a, vt, 0.0).astype(BF16)
        vct_out[0, 1, :, cols] = jnp.where(dim_a, 0.0, vt).astype(BF16)


def _cv_kernel(*refs, tq, seq, has_halo, final_norm):
    refs = list(refs)
    x_ref = refs.pop(0)
    if has_halo:
        xp_ref = refs.pop(0)
        xn_ref = refs.pop(0)
    (mod_ref, g_ref, wab_ref, wgz_ref, wout_ref, dww_ref, dwb_ref, lng_ref, lnb_ref) = refs[:9]
    refs = refs[9:]
    if final_norm:
        fg_ref = refs.pop(0)
    o_ref, h_s, zab_s, u_s, gz_s, c_s, yb_s = refs
    halo = CONV_HALO
    i = pl.program_id(1) if has_halo else 0
    t0 = i * tq

    shift, scale1, gate = _mod_params(mod_ref)
    g = g_ref[...]
    _modnorm_rows(h_s, halo, x_ref, tq, 128, g, scale1, shift)
    if has_halo:
        _modnorm_rows(h_s, 0, xp_ref, halo, halo, g, scale1, shift)
        _modnorm_rows(h_s, halo + tq, xn_ref, halo, halo, g, scale1, shift)
        zab_s[...] = jnp.dot(h_s[...], wab_ref[...], preferred_element_type=F32)
        z_rows, z_off = tq + 2 * halo, 0
    else:
        zab_s[halo:halo + tq, :] = jnp.dot(h_s[halo:halo + tq, :], wab_ref[...],
                                           preferred_element_type=F32)
        zero_halo = jnp.zeros((halo, D_MODEL), F32)
        u_s[0:halo, :] = zero_halo
        u_s[halo + tq:2 * halo + tq, :] = zero_halo
        z_rows, z_off = tq, halo
    gz_s[...] = jnp.dot(h_s[halo:halo + tq, :], wgz_ref[...], preferred_element_type=F32)

    gr = 32

    def glu_body(j, carry):
        r = pl.multiple_of(z_off + j * gr, 16)
        rows = pl.ds(r, gr)
        u = zab_s[rows, 0:D_MODEL] * _sigmoid(zab_s[rows, D_MODEL:2 * D_MODEL])
        pos = t0 - halo + r + lax.broadcasted_iota(jnp.int32, (gr, D_MODEL), 0)
        u_s[rows, :] = jnp.where((pos >= 0) & (pos < seq), u, 0.0)
        return carry

    lax.fori_loop(0, z_rows // gr, glu_body, 0)

    cr = 64
    win_rows = cr + 2 * halo

    def conv_body(j, carry):
        r = pl.multiple_of(j * cr, cr)
        for c in range(D_MODEL // LANES):
            sl = slice(c * LANES, (c + 1) * LANES)
            win = u_s[pl.ds(r, win_rows), sl]
            acc = None
            for phase in range(SUBLANES):
                shifted = win if phase == 0 else pltpu.roll(win, win_rows - phase, 0)
                for a in range(win_rows // SUBLANES):
                    k = a * SUBLANES + phase - 1
                    if 0 <= k < CONV_K:
                        term = shifted[a * SUBLANES:a * SUBLANES + cr] * dww_ref[k:k + 1, sl]
                        acc = term if acc is None else acc + term
            c_s[pl.ds(r, cr), sl] = acc + dwb_ref[:, sl]
        return carry

    lax.fori_loop(0, tq // cr, conv_body, 0)

    lng = lng_ref[...]
    lnb = lnb_ref[...]
    lr = 64

    def ln_body(j, carry):
        r = pl.multiple_of(j * lr, lr)
        rows = pl.ds(r, lr)
        y = c_s[rows, :]
        mu = jnp.mean(y, axis=-1, keepdims=True)
        yc = y - mu
        yn = yc * lax.rsqrt(jnp.mean(yc * yc, axis=-1, keepdims=True) + EPS)
        u2 = _silu(yn * lng + lnb)
        yb_s[rows, :] = (u2 * _silu(gz_s[rows, :])).astype(BF16)
        return carry

    lax.fori_loop(0, tq // lr, ln_body, 0)

    out = jnp.dot(yb_s[...], wout_ref[...], preferred_element_type=F32)
    x_new = x_ref[0] + gate * out
    if final_norm:
        ms = jnp.mean(x_new * x_new, axis=-1, keepdims=True)
        x_new = (x_new * lax.rsqrt(ms + EPS)) * fg_ref[...]
    o_ref[0] = x_new


def _ada_kernel(c_ref, w_ref, b_ref, o_ref):
    c = c_ref[...]
    s = c * jax.nn.sigmoid(c)
    o_ref[0] = jnp.dot(s, w_ref[0], precision=lax.Precision.HIGHEST,
                       preferred_element_type=F32) + b_ref[0]


def _ada_all(cond, ada_w, ada_b):
    n_layers = ada_w.shape[0]
    tn = 768
    return pl.pallas_call(
        _ada_kernel,
        grid=(n_layers, 3 * D_MODEL // tn),
        in_specs=[pl.BlockSpec((N_COND_ROWS, D_MODEL), lambda l, n: (0, 0)),
                  pl.BlockSpec((1, D_MODEL, tn), lambda l, n: (l, 0, n)),
                  pl.BlockSpec((1, 1, tn), lambda l, n: (l, 0, n))],
        out_specs=pl.BlockSpec((1, N_COND_ROWS, tn), lambda l, n: (l, 0, n)),
        out_shape=jax.ShapeDtypeStruct((n_layers, N_COND_ROWS, 3 * D_MODEL), F32),
        compiler_params=pltpu.CompilerParams(dimension_semantics=("arbitrary", "arbitrary"),
                                             vmem_limit_bytes=VMEM_LIMIT),
        name="adaln",
    )(cond, ada_w, ada_b.reshape(n_layers, 1, 3 * D_MODEL))


def _full(shape):
    return pl.BlockSpec(shape, lambda *_: (0,) * len(shape))


def _ap_scratch(tq, halo):
    n_kv = tq + 2 * halo
    return [pltpu.VMEM((n_kv, D_MODEL), BF16),
            pltpu.VMEM((tq, ATTN_DIM), F32),
            pltpu.VMEM((n_kv, 2 * KV_DIM), F32),
            pltpu.VMEM((tq, ATTN_DIM), BF16),
            pltpu.VMEM((2, n_kv, KV_DIM), BF16),
            pltpu.VMEM((2, n_kv // ATTN_BLOCK, KV_DIM, ATTN_BLOCK), BF16),
            pltpu.VMEM((tq + 2 * POOL_HALO, POOL_DIM), F32),
            pltpu.VMEM((tq, D_MODEL), F32),
            pltpu.VMEM((tq, D_MODEL), F32),
            pltpu.VMEM((tq, POOL_DIM), BF16),
            pltpu.VMEM((tq, D_MODEL), BF16)]


def _ap_latent(x, mod, w, sink, rope, kc, vct, tq):
    batch, seq, _ = x.shape
    bpt = tq // WINDOW
    n_blk = seq // WINDOW
    n_ctx = kc.shape[2]
    main = lambda b, i: (b, i, 0)
    prev = lambda b, i: (b, jnp.maximum(i * bpt - 1, 0), 0)
    nxt = lambda b, i: (b, jnp.minimum((i + 1) * bpt, n_blk - 1), 0)
    tmain = lambda b, i: (i, 0)
    tprev = lambda b, i: (jnp.maximum(i * bpt - 1, 0), 0)
    tnxt = lambda b, i: (jnp.minimum((i + 1) * bpt, n_blk - 1), 0)
    cos, sin = rope
    in_specs = [
        pl.BlockSpec(memory_space=pltpu.SMEM),
        pl.BlockSpec((1, tq, D_MODEL), main),
        pl.BlockSpec((1, WINDOW, D_MODEL), prev),
        pl.BlockSpec((1, WINDOW, D_MODEL), nxt),
        pl.BlockSpec((1, 1, 3 * D_MODEL), lambda b, i: (b, 0, 0)),
        _full((1, D_MODEL)),
        _full((D_MODEL, ATTN_DIM)), _full((D_MODEL, 2 * KV_DIM)), _full((D_MODEL, POOL_DIM)),
        _full((D_MODEL, D_MODEL)), _full((D_MODEL, D_MODEL)),
        _full((len(POOL_WINDOWS), POOL_GROUP, POOL_GROUP)), _full((1, POOL_DIM)),
        pl.BlockSpec((tq, LANES), tmain), pl.BlockSpec((tq, LANES), tmain),
        pl.BlockSpec((WINDOW, LANES), tprev), pl.BlockSpec((WINDOW, LANES), tprev),
        pl.BlockSpec((WINDOW, LANES), tnxt), pl.BlockSpec((WINDOW, LANES), tnxt),
        pl.BlockSpec((1, 2, n_ctx, KV_DIM), lambda b, i: (b, 0, 0, 0)),
        pl.BlockSpec((1, 2, KV_DIM, n_ctx), lambda b, i: (b, 0, 0, 0)),
    ]
    return pl.pallas_call(
        functools.partial(_ap_kernel, tq=tq, seq=seq, latent=True),
        grid=(batch, seq // tq),
        in_specs=in_specs,
        out_specs=pl.BlockSpec((1, tq, D_MODEL), main),
        out_shape=jax.ShapeDtypeStruct(x.shape, F32),
        scratch_shapes=_ap_scratch(tq, WINDOW),
        compiler_params=pltpu.CompilerParams(dimension_semantics=("arbitrary", "arbitrary"),
                                             vmem_limit_bytes=VMEM_LIMIT),
        name="ap_latent",
    )(sink, x, x, x, mod, w["norm_g"], w["wq"], w["wkv"], w["wu"], w["wgz"], w["wout"],
      w["pool_w"], w["pool_scale"], cos, sin, cos, sin, cos, sin, kc, vct)


def _ctx_kv_specs(batch, n):
    kc_spec = pl.BlockSpec((1, 2, n, KV_DIM), lambda b: (b, 0, 0, 0))
    vct_spec = pl.BlockSpec((1, 2, KV_DIM, n), lambda b: (b, 0, 0, 0))
    kc_shape = jax.ShapeDtypeStruct((batch, 2, n, KV_DIM), BF16)
    vct_shape = jax.ShapeDtypeStruct((batch, 2, KV_DIM, n), BF16)
    return [kc_spec, vct_spec], [kc_shape, vct_shape]


def _ap_ctx(xc, mod_ctx, w, sink):
    batch, n, _ = xc.shape
    row = lambda b: (b, 0, 0)
    kv_specs, kv_shapes = _ctx_kv_specs(batch, n)
    in_specs = [
        pl.BlockSpec(memory_space=pltpu.SMEM),
        pl.BlockSpec((1, n, D_MODEL), row),
        _full((1, 1, 3 * D_MODEL)),
        _full((1, D_MODEL)),
        _full((D_MODEL, ATTN_DIM)), _full((D_MODEL, 2 * KV_DIM)), _full((D_MODEL, POOL_DIM)),
        _full((D_MODEL, D_MODEL)), _full((D_MODEL, D_MODEL)),
        _full((len(POOL_WINDOWS), POOL_GROUP, POOL_GROUP)), _full((1, POOL_DIM)),
    ]
    return pl.pallas_call(
        functools.partial(_ap_kernel, tq=n, seq=n, latent=False),
        grid=(batch,),
        in_specs=in_specs,
        out_specs=[pl.BlockSpec((1, n, D_MODEL), row)] + kv_specs,
        out_shape=[jax.ShapeDtypeStruct(xc.shape, F32)] + kv_shapes,
        scratch_shapes=_ap_scratch(n, 0),
        compiler_params=pltpu.CompilerParams(dimension_semantics=("arbitrary",),
                                             vmem_limit_bytes=VMEM_LIMIT),
        name="ap_ctx",
    )(sink, xc, mod_ctx, w["norm_g"], w["wq"], w["wkv"], w["wu"], w["wgz"], w["wout"],
      w["pool_w"], w["pool_scale"])


def _ctx_kv(xc, mod_ctx, w):
    batch, n, _ = xc.shape
    kv_specs, kv_shapes = _ctx_kv_specs(batch, n)
    return pl.pallas_call(
        functools.partial(_ctx_kv_kernel, n_rows=n),
        grid=(batch,),
        in_specs=[pl.BlockSpec((1, n, D_MODEL), lambda b: (b, 0, 0)),
                  _full((1, 1, 3 * D_MODEL)), _full((1, D_MODEL)), _full((D_MODEL, 2 * KV_DIM))],
        out_specs=kv_specs,
        out_shape=kv_shapes,
        scratch_shapes=[pltpu.VMEM((n, D_MODEL), BF16)],
        compiler_params=pltpu.CompilerParams(dimension_semantics=("arbitrary",),
                                             vmem_limit_bytes=VMEM_LIMIT),
        name="ctx_kv",
    )(xc, mod_ctx, w["norm_g"], w["wkv"])


def _cv_layer(x, mod, w, tq, has_halo, final_g=None):
    batch, seq, _ = x.shape
    halo = CONV_HALO
    final_norm = final_g is not None
    if has_halo:
        bpt = tq // halo
        n_blk = seq // halo
        grid = (batch, seq // tq)
        main = lambda b, i: (b, i, 0)
        x_specs = [pl.BlockSpec((1, tq, D_MODEL), main),
                   pl.BlockSpec((1, halo, D_MODEL), lambda b, i: (b, jnp.maximum(i * bpt - 1, 0), 0)),
                   pl.BlockSpec((1, halo, D_MODEL),
                                lambda b, i: (b, jnp.minimum((i + 1) * bpt, n_blk - 1), 0))]
        x_args = [x, x, x]
        mod_spec = pl.BlockSpec((1, 1, 3 * D_MODEL), lambda b, i: (b, 0, 0))
        sem = ("arbitrary", "arbitrary")
    else:
        assert tq == seq
        grid = (batch,)
        main = lambda b: (b, 0, 0)
        x_specs = [pl.BlockSpec((1, tq, D_MODEL), main)]
        x_args = [x]
        mod_spec = _full((1, 1, 3 * D_MODEL))
        sem = ("arbitrary",)
    in_specs = x_specs + [
        mod_spec, _full((1, D_MODEL)),
        _full((D_MODEL, 2 * D_MODEL)), _full((D_MODEL, D_MODEL)), _full((D_MODEL, D_MODEL)),
        _full((CONV_K, D_MODEL)), _full((1, D_MODEL)), _full((1, D_MODEL)), _full((1, D_MODEL)),
    ]
    args = x_args + [mod, w["norm_g"], w["wab"], w["wgz"], w["wout"], w["dw_w"], w["dw_b"],
                     w["ln_g"], w["ln_b"]]
    if final_norm:
        in_specs.append(_full((1, D_MODEL)))
        args.append(final_g)
    scratch = [pltpu.VMEM((tq + 2 * halo, D_MODEL), BF16),
               pltpu.VMEM((tq + 2 * halo, 2 * D_MODEL), F32),
               pltpu.VMEM((tq + 2 * halo, D_MODEL), F32),
               pltpu.VMEM((tq, D_MODEL), F32),
               pltpu.VMEM((tq, D_MODEL), F32),
               pltpu.VMEM((tq, D_MODEL), BF16)]
    return pl.pallas_call(
        functools.partial(_cv_kernel, tq=tq, seq=seq, has_halo=has_halo, final_norm=final_norm),
        grid=grid,
        in_specs=in_specs,
        out_specs=pl.BlockSpec((1, tq, D_MODEL), main),
        out_shape=jax.ShapeDtypeStruct(x.shape, F32),
        scratch_shapes=scratch,
        compiler_params=pltpu.CompilerParams(dimension_semantics=sem,
                                             vmem_limit_bytes=VMEM_LIMIT),
        name="cv_latent" if has_halo else "cv_ctx",
    )(*args)


def _rope_tables(seq):
    rows = seq // GRID_W
    row = jnp.repeat(jnp.arange(rows, dtype=F32), GRID_W)
    col = jnp.tile(jnp.arange(GRID_W, dtype=F32), rows)
    inv_freq = ROPE_THETA ** (-jnp.arange(ROPE_AXIS_PAIRS, dtype=F32) / ROPE_AXIS_PAIRS)
    ang = jnp.concatenate([row[:, None] * inv_freq, col[:, None] * inv_freq], axis=-1)
    cos, sin = jnp.cos(ang), jnp.sin(ang)
    cos_slab = jnp.tile(cos, (1, 4))
    sin_slab = jnp.concatenate([-sin, -sin, sin, sin], axis=-1)
    return cos_slab, sin_slab


def _rope_slab_perm(head_a, head_b):
    half = HEAD_DIM // 2
    cols = []
    for part in range(2):
        for head in (head_a, head_b):
            cols.extend(range(head * HEAD_DIM + part * half, head * HEAD_DIM + (part + 1) * half))
    return cols


def _q_perm():
    return np.asarray(sum((_rope_slab_perm(c, 4 + c) for c in range(4)), []), dtype=np.int32)


def _attn_out_perm():
    perm = []
    for c in range(4):
        for head in (c, 4 + c):
            perm.extend(range(head * HEAD_DIM, (head + 1) * HEAD_DIM))
    return np.asarray(perm, dtype=np.int32)


def _ap_weights(j, ap_norm_g, ap_w_in, ap_w_out, ap_pool_w, ap_pool_scale):
    q_perm = _q_perm()
    k_perm = np.asarray(_rope_slab_perm(0, 1), dtype=np.int32)
    out_perm = _attn_out_perm()
    w_in = ap_w_in[j]
    o_k = ATTN_DIM
    o_v = ATTN_DIM + KV_DIM
    o_u = ATTN_DIM + 2 * KV_DIM
    o_gz = o_u + POOL_DIM
    wq = w_in[:, :ATTN_DIM][:, q_perm] * (HEAD_DIM ** -0.5)
    wkv = jnp.concatenate([w_in[:, o_k:o_v][:, k_perm], w_in[:, o_v:o_u]], axis=1)
    wgz = w_in[:, o_gz:]
    wgz = jnp.concatenate([wgz[:, :ATTN_DIM][:, out_perm], wgz[:, ATTN_DIM:]], axis=1)
    w_out = ap_w_out[j]
    w_out = jnp.concatenate([w_out[:ATTN_DIM][out_perm], w_out[ATTN_DIM:]], axis=0)
    return {
        "norm_g": ap_norm_g[j].reshape(1, D_MODEL),
        "wq": wq.astype(BF16),
        "wkv": wkv.astype(BF16),
        "wu": w_in[:, o_u:o_gz].astype(BF16),
        "wgz": wgz.astype(BF16),
        "wout": w_out.astype(BF16),
        "pool_w": ap_pool_w[j].astype(BF16),
        "pool_scale": ap_pool_scale[j].reshape(1, POOL_DIM),
    }


def _cv_weights(j, cv_norm_g, cv_w_in, cv_w_out, cv_dw_w, cv_dw_b, cv_ln_g, cv_ln_b):
    w_in = cv_w_in[j]
    return {
        "norm_g": cv_norm_g[j].reshape(1, D_MODEL),
        "wab": w_in[:, :2 * D_MODEL].astype(BF16),
        "wgz": w_in[:, 2 * D_MODEL:].astype(BF16),
        "wout": cv_w_out[j].astype(BF16),
        "dw_w": cv_dw_w[j],
        "dw_b": cv_dw_b[j].reshape(1, D_MODEL),
        "ln_g": cv_ln_g[j].reshape(1, D_MODEL),
        "ln_b": cv_ln_b[j].reshape(1, D_MODEL),
    }


def kernel(x, c, ctx, c_ctx, ap_ada_w, ap_ada_b, ap_norm_g, ap_w_in, ap_w_out, ap_sink, ap_pool_w, ap_pool_scale, cv_ada_w, cv_ada_b, cv_norm_g, cv_w_in, cv_w_out, cv_dw_w, cv_dw_b, cv_ln_g, cv_ln_b, final_norm_g):
    batch, seq, _ = x.shape
    tq = 512
    cond = jnp.zeros((N_COND_ROWS, D_MODEL), F32).at[:batch].set(c).at[batch].set(c_ctx)
    ap_mod = _ada_all(cond, ap_ada_w, ap_ada_b).reshape(-1, N_COND_ROWS, 1, 3 * D_MODEL)
    cv_mod = _ada_all(cond, cv_ada_w, cv_ada_b).reshape(-1, N_COND_ROWS, 1, 3 * D_MODEL)
    rope = _rope_tables(seq)
    last_ap_layer = ((DEPTH - 1) // 2) * 2
    xc = ctx
    for i in range(DEPTH):
        j = i // 2
        update_ctx = i < last_ap_layer
        if i % 2 == 0:
            w = _ap_weights(j, ap_norm_g, ap_w_in, ap_w_out, ap_pool_w, ap_pool_scale)
            mod = ap_mod[j]
            mod_ctx = mod[batch:batch + 1]
            if update_ctx:
                xc, kc, vct = _ap_ctx(xc, mod_ctx, w, ap_sink[j])
            else:
                kc, vct = _ctx_kv(xc, mod_ctx, w)
            x = _ap_latent(x, mod, w, ap_sink[j], rope, kc, vct, tq)
        else:
            w = _cv_weights(j, cv_norm_g, cv_w_in, cv_w_out, cv_dw_w, cv_dw_b, cv_ln_g, cv_ln_b)
            mod = cv_mod[j]
            if update_ctx:
                xc = _cv_layer(xc, mod[batch:batch + 1], w, xc.shape[1], has_halo=False)
            final_g = final_norm_g.reshape(1, D_MODEL) if i == DEPTH - 1 else None
            x = _cv_layer(x, mod, w, tq, has_halo=True, final_g=final_g)
    return x
```

```python
import functools

import jax
import jax.numpy as jnp
import numpy as np
from jax import lax
from jax.experimental import pallas as pl
from jax.experimental.pallas import tpu as pltpu

F32 = jnp.float32
BF16 = jnp.bfloat16

D_MODEL = 1024
DEPTH = 4
GRID_W = 64
HEAD_DIM = 64
ATTN_DIM = 512
N_Q_HEADS = 8
KV_DIM = 128
WINDOW = 128
ATTN_BLOCK = 128
ROPE_THETA = 10000.0
ROPE_AXIS_PAIRS = HEAD_DIM // 4
POOL_WINDOWS = (2, 4, 8, 16)
POOL_DIM = 512
POOL_GROUP = 128
POOL_HALO = 8
CONV_K = 31
CONV_HALO = 16
CV_SUB_ROWS = 128
CONV_CHUNK = 64
AP_SUB_ROWS = 256
POOL_CHUNK = 64
EPS = 1e-6
NEG_INF = -1e30
LOG2_E = 1.4426950408889634

LANES = 128
SUBLANES = 8
BF16_ROWS = 16
N_COND_ROWS = 16
VMEM_LIMIT = 56 * 1024 * 1024
NT_DIMS = (((1,), (1,)), ((), ()))


def _sigmoid(x):
    return 0.5 * jnp.tanh(0.5 * x) + 0.5


def _silu(x):
    return x * _sigmoid(x)


def _modnorm(x, gain, shift):
    ms = jnp.mean(x * x, axis=-1, keepdims=True)
    return (x * lax.rsqrt(ms + EPS)) * gain + shift


def _modnorm_rows(dst_ref, dst_off, src_ref, n_rows, chunk, gain, shift):
    def body(j, carry):
        r = pl.multiple_of(j * chunk, chunk)
        x = src_ref[0, pl.ds(r, chunk), :]
        dst_ref[pl.ds(pl.multiple_of(dst_off + r, BF16_ROWS), chunk), :] = (
            _modnorm(x, gain, shift).astype(BF16))
        return carry

    lax.fori_loop(0, n_rows // chunk, body, 0)


def _mod_params(mod_ref, g_ref):
    shift = mod_ref[0, :, 0:D_MODEL]
    gain = g_ref[...] * (1.0 + mod_ref[0, :, D_MODEL:2 * D_MODEL])
    gate = mod_ref[0, :, 2 * D_MODEL:3 * D_MODEL]
    return shift, gain, gate


def _rope(x, cos, sin):
    return x * cos + pltpu.roll(x, 64, 1) * sin


def _head_a_lanes(rows):
    lane = lax.broadcasted_iota(jnp.int32, (rows, LANES), 1)
    return (lane & 32) == 0


def _scores(q_stack, k_blocks, masks):
    scores = []
    for k, mask in zip(k_blocks, masks):
        s = lax.dot_general(k, q_stack, NT_DIMS, preferred_element_type=F32)
        if mask is not None:
            s = jnp.where(mask, s, NEG_INF)
        scores.append(s)
    return scores


def _softmax(scores, sink_row):
    m = sink_row
    for s in scores:
        m = jnp.maximum(m, jnp.max(s, axis=0, keepdims=True))
    denom = jnp.exp2(sink_row - m)
    probs = []
    for s in scores:
        p = jnp.exp2(s - m)
        denom = denom + jnp.sum(p, axis=0, keepdims=True)
        probs.append(p.astype(BF16))
    return probs, denom


def _sink_row(sink_ref, kv_head):
    return jnp.concatenate(
        [jnp.full((1, LANES), sink_ref[c + 4 * kv_head] * LOG2_E, F32) for c in range(4)], axis=1)


def _ap_kernel(*refs, tq, seq, latent):
    if latent:
        (sink_ref, x_ref, xp_ref, xn_ref, mod_ref, g_ref, wq_ref, wkv_ref, wu_ref, wgz_ref,
         wout_ref, pw_ref, ps_ref, cos_ref, sin_ref, cosp_ref, sinp_ref, cosn_ref, sinn_ref,
         kc_ref, vct_ref, o_ref,
         h_s, q_s, kl_s, vt_s, u_s, gz_s, y_s) = refs
        halo = WINDOW
        i = pl.program_id(1)
        n_tiles = seq // tq
    else:
        (sink_ref, x_ref, mod_ref, g_ref, wq_ref, wkv_ref, wu_ref, wgz_ref,
         wout_ref, pw_ref, ps_ref, o_ref, kc_out, vct_out,
         h_s, q_s, kl_s, vt_s, u_s, gz_s, y_s) = refs
        halo = 0
        i = 0
        n_tiles = 1
    t0 = i * tq
    n_blocks = tq // ATTN_BLOCK
    halo_blocks = halo // ATTN_BLOCK
    sub = AP_SUB_ROWS
    n_sub = tq // sub
    blocks_per_sub = sub // ATTN_BLOCK

    shift, gain, gate = _mod_params(mod_ref, g_ref)
    head_a = _head_a_lanes(ATTN_BLOCK)
    dim_a = lax.broadcasted_iota(jnp.int32, (KV_DIM, ATTN_BLOCK), 0) < HEAD_DIM

    def norm_rows(dst_lo, src_ref, src_lo, n):
        h_s[dst_lo:dst_lo + n, :] = _modnorm(src_ref[0, src_lo:src_lo + n, :], gain,
                                             shift).astype(BF16)

    def project_kv(h_lo, n, c_ref, s_ref, t_lo):
        z = jnp.dot(h_s[h_lo:h_lo + n, :], wkv_ref[...], preferred_element_type=F32)
        for b in range(n // ATTN_BLOCK):
            zr = slice(b * ATTN_BLOCK, (b + 1) * ATTN_BLOCK)
            hr = slice(h_lo + b * ATTN_BLOCK, h_lo + (b + 1) * ATTN_BLOCK)
            tr = slice(t_lo + b * ATTN_BLOCK, t_lo + (b + 1) * ATTN_BLOCK)
            k = z[zr, 0:LANES]
            if latent:
                k = _rope(k, c_ref[tr, :], s_ref[tr, :])
            kl_s[0, hr, :] = jnp.where(head_a, k, 0.0).astype(BF16)
            kl_s[1, hr, :] = jnp.where(head_a, 0.0, k).astype(BF16)
            vt = z[zr, LANES:2 * LANES].T
            blk = h_lo // ATTN_BLOCK + b
            vt_s[0, blk] = jnp.where(dim_a, vt, 0.0).astype(BF16)
            vt_s[1, blk] = jnp.where(dim_a, 0.0, vt).astype(BF16)

    def project_main(sc):
        rows = slice(sc * sub, (sc + 1) * sub)
        hm = h_s[halo + sc * sub:halo + (sc + 1) * sub, :]
        u_s[POOL_HALO + sc * sub:POOL_HALO + (sc + 1) * sub, :] = jnp.dot(
            hm, wu_ref[...], preferred_element_type=F32)
        zq = jnp.dot(hm, wq_ref[...], preferred_element_type=F32)
        for c in range(4):
            sl = slice(c * LANES, (c + 1) * LANES)
            q = zq[:, sl]
            if latent:
                q = _rope(q, cos_ref[rows, :], sin_ref[rows, :])
            q_s[rows, sl] = q.astype(BF16)
        gz_s[rows, :] = jnp.dot(hm, wgz_ref[...], preferred_element_type=F32)

    def pool_halos():
        zero_halo = jnp.zeros((POOL_HALO, POOL_DIM), F32)
        if latent:
            up = jnp.dot(h_s[halo - BF16_ROWS:halo, :], wu_ref[...], preferred_element_type=F32)
            un = jnp.dot(h_s[halo + tq:halo + tq + BF16_ROWS, :], wu_ref[...],
                         preferred_element_type=F32)
            u_s[0:POOL_HALO, :] = jnp.where(i > 0, up[BF16_ROWS - POOL_HALO:BF16_ROWS], zero_halo)
            u_s[POOL_HALO + tq:2 * POOL_HALO + tq, :] = jnp.where(i < n_tiles - 1,
                                                                   un[0:POOL_HALO], zero_halo)
        else:
            u_s[0:POOL_HALO, :] = zero_halo
            u_s[POOL_HALO + tq:2 * POOL_HALO + tq, :] = zero_halo

    def pool(sc):
        win_rows = POOL_CHUNK + 2 * POOL_HALO
        diffs = []
        for r in range(sc * sub, (sc + 1) * sub, POOL_CHUNK):
            row = []
            for gi, w in enumerate(POOL_WINDOWS):
                sl = slice(gi * POOL_GROUP, (gi + 1) * POOL_GROUP)
                win = u_s[r:r + win_rows, sl]
                acc = win + pltpu.roll(win, 1, 0)
                span = 2
                while span < w:
                    acc = pltpu.roll(acc, span // 2, 0) + pltpu.roll(acc, win_rows - span // 2, 0)
                    span *= 2
                acc = acc[POOL_HALO:POOL_HALO + POOL_CHUNK]
                if 0 < r < tq - POOL_CHUNK:
                    mean = acc * (1.0 / w)
                else:
                    t = t0 + r + lax.broadcasted_iota(jnp.int32, (POOL_CHUNK, POOL_GROUP), 0)
                    cnt = (jnp.minimum(t + w // 2, seq) - jnp.maximum(t - w // 2, 0)).astype(F32)
                    mean = acc / cnt
                diff = mean - win[POOL_HALO:POOL_HALO + POOL_CHUNK]
                row.append(diff.astype(BF16))
            diffs.append(row)
        for gi in range(len(POOL_WINDOWS)):
            sl = slice(gi * POOL_GROUP, (gi + 1) * POOL_GROUP)
            d = jnp.concatenate([row[gi] for row in diffs], axis=0)
            yg = jnp.dot(d, pw_ref[gi], preferred_element_type=F32) * ps_ref[:, sl]
            y_s[sc * sub:(sc + 1) * sub, ATTN_DIM + gi * POOL_GROUP:ATTN_DIM + (gi + 1) * POOL_GROUP] = yg

    if latent:
        kk = lax.broadcasted_iota(jnp.int32, (ATTN_BLOCK, 4 * ATTN_BLOCK), 0)
        qq = lax.broadcasted_iota(jnp.int32, (ATTN_BLOCK, 4 * ATTN_BLOCK), 1) & (ATTN_BLOCK - 1)
        no_prev = jnp.where(i > 0, 0, ATTN_BLOCK)
        no_next = jnp.where(i < n_tiles - 1, 0, ATTN_BLOCK)

    out_a = lax.broadcasted_iota(jnp.int32, (KV_DIM, 4 * ATTN_BLOCK), 0) < HEAD_DIM

    def block_scores(j):
        r = j * ATTN_BLOCK
        q_stack = jnp.concatenate([q_s[r:r + ATTN_BLOCK, c * LANES:(c + 1) * LANES] for c in range(4)],
                                  axis=0)
        if latent:
            prev_ok = kk >= (qq + no_prev if j == 0 else qq)
            next_ok = kk <= (qq - no_next if j == n_blocks - 1 else qq)
            masks = [prev_ok, None, next_ok, None]
        else:
            masks = [None]
        out = []
        for kvh in range(2):
            if latent:
                k_blocks = [kl_s[kvh, r + b * ATTN_BLOCK:r + (b + 1) * ATTN_BLOCK, :] for b in range(3)]
                k_blocks.append(kc_ref[0, kvh])
            else:
                k_blocks = [kl_s[kvh]]
            out.append(_scores(q_stack, k_blocks, masks))
        return out

    def block_finish(j, scores):
        r = j * ATTN_BLOCK
        probs, vts, denoms = [], [], []
        for kvh in range(2):
            p, denom = _softmax(scores[kvh], _sink_row(sink_ref, kvh))
            probs.extend(p)
            denoms.append(denom)
            if latent:
                vts.extend([vt_s[kvh, j + b] for b in range(3)] + [vct_ref[0, kvh]])
            else:
                vts.extend([vt_s[kvh, b] for b in range(n_blocks)])
        acc = jnp.dot(jnp.concatenate(vts, axis=1), jnp.concatenate(probs, axis=0),
                      preferred_element_type=F32)
        acc = acc / jnp.where(out_a, denoms[0], denoms[1])
        for c in range(4):
            y_s[r:r + ATTN_BLOCK, c * LANES:(c + 1) * LANES] = acc[:, c * ATTN_BLOCK:(c + 1) * ATTN_BLOCK].T

    def emit(sc):
        rows = slice(sc * sub, (sc + 1) * sub)
        yb = (y_s[rows, :] * _silu(gz_s[rows, :])).astype(BF16)
        out = jnp.dot(yb, wout_ref[...], preferred_element_type=F32)
        o_ref[0, rows, :] = x_ref[0, rows, :] + gate * out

    if latent:
        norm_rows(0, xp_ref, 0, halo)
    norm_rows(halo, x_ref, 0, sub)
    if latent:
        project_kv(0, halo, cosp_ref, sinp_ref, 0)
    for sc in range(n_sub):
        if sc + 1 < n_sub:
            norm_rows(halo + (sc + 1) * sub, x_ref, (sc + 1) * sub, sub)
        elif latent:
            norm_rows(halo + tq, xn_ref, 0, halo)
        project_kv(halo + sc * sub, sub, cos_ref if latent else None, sin_ref if latent else None,
                   sc * sub)
        project_main(sc)
    if latent:
        project_kv(halo + tq, halo, cosn_ref, sinn_ref, 0)
    pool_halos()
    if not latent:
        kc_out[0] = kl_s[...]
        for kvh in range(2):
            for b in range(n_blocks):
                vct_out[0, kvh, :, b * ATTN_BLOCK:(b + 1) * ATTN_BLOCK] = vt_s[kvh, b]

    scores = block_scores(0)
    for j in range(n_blocks):
        sc, first_of_sub = divmod(j, blocks_per_sub)
        if first_of_sub == 0:
            pool(sc)
        nxt = block_scores(j + 1) if j + 1 < n_blocks else None
        block_finish(j, scores)
        scores = nxt
        if (j + 1) % blocks_per_sub == 0:
            emit(sc)


def _ctx_kv_kernel(x_ref, mod_ref, g_ref, wkv_ref, kc_out, vct_out, h_s, *, n_rows):
    shift, gain, _ = _mod_params(mod_ref, g_ref)
    _modnorm_rows(h_s, 0, x_ref, n_rows, 128, gain, shift)
    z = jnp.dot(h_s[...], wkv_ref[...], preferred_element_type=F32)
    head_a = _head_a_lanes(n_rows)
    k = z[:, 0:LANES]
    kc_out[0, 0] = jnp.where(head_a, k, 0.0).astype(BF16)
    kc_out[0, 1] = jnp.where(head_a, 0.0, k).astype(BF16)
    dim_a = lax.broadcasted_iota(jnp.int32, (KV_DIM, ATTN_BLOCK), 0) < HEAD_DIM
    for b in range(n_rows // ATTN_BLOCK):
        cols = slice(b * ATTN_BLOCK, (b + 1) * ATTN_BLOCK)
        vt = z[cols, LANES:2 * LANES].T
        vct_out[0, 0, :, cols] = jnp.where(dim_a, vt, 0.0).astype(BF16)
        vct_out[0, 1, :, cols] = jnp.where(dim_a, 0.0, vt).astype(BF16)


def _cv_kernel(*refs, tq, seq, has_halo, final_norm):
    refs = list(refs)
    x_ref = refs.pop(0)
    if has_halo:
        xp_ref = refs.pop(0)
        xn_ref = refs.pop(0)
    (mod_ref, g_ref, wab_ref, wgz_ref, wout_ref, dww_ref, dwb_ref, lng_ref, lnb_ref) = refs[:9]
    refs = refs[9:]
    if final_norm:
        fg_ref = refs.pop(0)
    o_ref, h_s, u_s, gz_s, c_s, yb_s = refs
    halo = CONV_HALO
    if has_halo:
        i = pl.program_id(1)
        n_tiles = seq // tq
    sub = CV_SUB_ROWS
    n_sub = tq // sub
    cr = CONV_CHUNK
    win_rows = cr + 2 * halo

    shift, gain, gate = _mod_params(mod_ref, g_ref)
    lng = lng_ref[...]
    lnb = lnb_ref[...]
    if not has_halo:
        zero_halo = jnp.zeros((halo, D_MODEL), F32)
        u_s[0:halo, :] = zero_halo
        u_s[halo + tq:2 * halo + tq, :] = zero_halo

    def project(sc):
        lo = halo + sc * sub
        hi = lo + sub
        h_s[lo:hi, :] = _modnorm(x_ref[0, sc * sub:(sc + 1) * sub, :], gain, shift).astype(BF16)
        zlo, zhi = lo, hi
        if has_halo and sc == 0:
            h_s[0:halo, :] = _modnorm(xp_ref[0], gain, shift).astype(BF16)
            zlo = 0
        if has_halo and sc == n_sub - 1:
            h_s[halo + tq:2 * halo + tq, :] = _modnorm(xn_ref[0], gain, shift).astype(BF16)
            zhi = tq + 2 * halo
        zab = jnp.dot(h_s[zlo:zhi, :], wab_ref[...], preferred_element_type=F32)
        u_s[zlo:zhi, :] = zab[:, 0:D_MODEL] * _sigmoid(zab[:, D_MODEL:2 * D_MODEL])
        if has_halo and sc == 0:
            u_s[0:halo, :] = jnp.where(i > 0, u_s[0:halo, :], 0.0)
        if has_halo and sc == n_sub - 1:
            u_s[halo + tq:2 * halo + tq, :] = jnp.where(i < n_tiles - 1,
                                                        u_s[halo + tq:2 * halo + tq, :], 0.0)
        gz_s[sc * sub:(sc + 1) * sub, :] = jnp.dot(h_s[lo:hi, :], wgz_ref[...],
                                                   preferred_element_type=F32)

    def mix(sc):
        for r in range(sc * sub, (sc + 1) * sub, cr):
            for c in range(D_MODEL // LANES):
                sl = slice(c * LANES, (c + 1) * LANES)
                win = u_s[r:r + win_rows, sl]
                acc = None
                for phase in range(SUBLANES):
                    shifted = win if phase == 0 else pltpu.roll(win, win_rows - phase, 0)
                    for a in range(win_rows // SUBLANES):
                        k = a * SUBLANES + phase - 1
                        if 0 <= k < CONV_K:
                            term = shifted[a * SUBLANES:a * SUBLANES + cr] * dww_ref[k:k + 1, sl]
                            acc = term if acc is None else acc + term
                c_s[r:r + cr, sl] = acc + dwb_ref[:, sl]
            y = c_s[r:r + cr, :]
            mu = jnp.mean(y, axis=-1, keepdims=True)
            yc = y - mu
            yn = yc * lax.rsqrt(jnp.mean(yc * yc, axis=-1, keepdims=True) + EPS)
            u2 = _silu(yn * lng + lnb)
            yb_s[r:r + cr, :] = (u2 * _silu(gz_s[r:r + cr, :])).astype(BF16)

    def emit(sc):
        rows = slice(sc * sub, (sc + 1) * sub)
        out = jnp.dot(yb_s[rows, :], wout_ref[...], preferred_element_type=F32)
        x_new = x_ref[0, rows, :] + gate * out
        if final_norm:
            ms = jnp.mean(x_new * x_new, axis=-1, keepdims=True)
            x_new = (x_new * lax.rsqrt(ms + EPS)) * fg_ref[...]
        o_ref[0, rows, :] = x_new

    for sc in range(min(2, n_sub)):
        project(sc)
    for sc in range(n_sub):
        if sc + 2 < n_sub:
            project(sc + 2)
        mix(sc)
        emit(sc)


def _ada_kernel(c_ref, w_ref, b_ref, o_ref):
    c = c_ref[...]
    s = c * jax.nn.sigmoid(c)
    o_ref[0] = jnp.dot(s, w_ref[0], precision=lax.Precision.HIGHEST,
                       preferred_element_type=F32) + b_ref[0]


def _ada_all(cond, ada_w, ada_b):
    n_layers = ada_w.shape[0]
    tn = 768
    return pl.pallas_call(
        _ada_kernel,
        grid=(n_layers, 3 * D_MODEL // tn),
        in_specs=[pl.BlockSpec((N_COND_ROWS, D_MODEL), lambda l, n: (0, 0)),
                  pl.BlockSpec((1, D_MODEL, tn), lambda l, n: (l, 0, n)),
                  pl.BlockSpec((1, 1, tn), lambda l, n: (l, 0, n))],
        out_specs=pl.BlockSpec((1, N_COND_ROWS, tn), lambda l, n: (l, 0, n)),
        out_shape=jax.ShapeDtypeStruct((n_layers, N_COND_ROWS, 3 * D_MODEL), F32),
        compiler_params=pltpu.CompilerParams(dimension_semantics=("arbitrary", "arbitrary"),
                                             vmem_limit_bytes=VMEM_LIMIT),
        name="adaln",
    )(cond, ada_w, ada_b.reshape(n_layers, 1, 3 * D_MODEL))


def _full(shape):
    return pl.BlockSpec(shape, lambda *_: (0,) * len(shape))


def _ap_scratch(tq, halo):
    n_kv = tq + 2 * halo
    return [pltpu.VMEM((n_kv, D_MODEL), BF16),
            pltpu.VMEM((tq, ATTN_DIM), BF16),
            pltpu.VMEM((2, n_kv, KV_DIM), BF16),
            pltpu.VMEM((2, n_kv // ATTN_BLOCK, KV_DIM, ATTN_BLOCK), BF16),
            pltpu.VMEM((tq + 2 * POOL_HALO, POOL_DIM), F32),
            pltpu.VMEM((tq, D_MODEL), F32),
            pltpu.VMEM((tq, D_MODEL), F32)]


def _ap_latent(x, mod, w, sink, rope, kc, vct, tq):
    batch, seq, _ = x.shape
    bpt = tq // WINDOW
    n_blk = seq // WINDOW
    n_ctx = kc.shape[2]
    main = lambda b, i: (b, i, 0)
    prev = lambda b, i: (b, jnp.maximum(i * bpt - 1, 0), 0)
    nxt = lambda b, i: (b, jnp.minimum((i + 1) * bpt, n_blk - 1), 0)
    tmain = lambda b, i: (i, 0)
    tprev = lambda b, i: (jnp.maximum(i * bpt - 1, 0), 0)
    tnxt = lambda b, i: (jnp.minimum((i + 1) * bpt, n_blk - 1), 0)
    cos, sin = rope
    in_specs = [
        pl.BlockSpec(memory_space=pltpu.SMEM),
        pl.BlockSpec((1, tq, D_MODEL), main),
        pl.BlockSpec((1, WINDOW, D_MODEL), prev),
        pl.BlockSpec((1, WINDOW, D_MODEL), nxt),
        pl.BlockSpec((1, 1, 3 * D_MODEL), lambda b, i: (b, 0, 0)),
        _full((1, D_MODEL)),
        _full((D_MODEL, ATTN_DIM)), _full((D_MODEL, 2 * KV_DIM)), _full((D_MODEL, POOL_DIM)),
        _full((D_MODEL, D_MODEL)), _full((D_MODEL, D_MODEL)),
        _full((len(POOL_WINDOWS), POOL_GROUP, POOL_GROUP)), _full((1, POOL_DIM)),
        pl.BlockSpec((tq, LANES), tmain), pl.BlockSpec((tq, LANES), tmain),
        pl.BlockSpec((WINDOW, LANES), tprev), pl.BlockSpec((WINDOW, LANES), tprev),
        pl.BlockSpec((WINDOW, LANES), tnxt), pl.BlockSpec((WINDOW, LANES), tnxt),
        pl.BlockSpec((1, 2, n_ctx, KV_DIM), lambda b, i: (b, 0, 0, 0)),
        pl.BlockSpec((1, 2, KV_DIM, n_ctx), lambda b, i: (b, 0, 0, 0)),
    ]
    return pl.pallas_call(
        functools.partial(_ap_kernel, tq=tq, seq=seq, latent=True),
        grid=(batch, seq // tq),
        in_specs=in_specs,
        out_specs=pl.BlockSpec((1, tq, D_MODEL), main),
        out_shape=jax.ShapeDtypeStruct(x.shape, F32),
        scratch_shapes=_ap_scratch(tq, WINDOW),
        compiler_params=pltpu.CompilerParams(dimension_semantics=("arbitrary", "arbitrary"),
                                             vmem_limit_bytes=VMEM_LIMIT),
        name="ap_latent",
    )(sink, x, x, x, mod, w["norm_g"], w["wq"], w["wkv"], w["wu"], w["wgz"], w["wout"],
      w["pool_w"], w["pool_scale"], cos, sin, cos, sin, cos, sin, kc, vct)


def _ctx_kv_specs(batch, n):
    kc_spec = pl.BlockSpec((1, 2, n, KV_DIM), lambda b: (b, 0, 0, 0))
    vct_spec = pl.BlockSpec((1, 2, KV_DIM, n), lambda b: (b, 0, 0, 0))
    kc_shape = jax.ShapeDtypeStruct((batch, 2, n, KV_DIM), BF16)
    vct_shape = jax.ShapeDtypeStruct((batch, 2, KV_DIM, n), BF16)
    return [kc_spec, vct_spec], [kc_shape, vct_shape]


def _ap_ctx(xc, mod_ctx, w, sink):
    batch, n, _ = xc.shape
    row = lambda b: (b, 0, 0)
    kv_specs, kv_shapes = _ctx_kv_specs(batch, n)
    in_specs = [
        pl.BlockSpec(memory_space=pltpu.SMEM),
        pl.BlockSpec((1, n, D_MODEL), row),
        _full((1, 1, 3 * D_MODEL)),
        _full((1, D_MODEL)),
        _full((D_MODEL, ATTN_DIM)), _full((D_MODEL, 2 * KV_DIM)), _full((D_MODEL, POOL_DIM)),
        _full((D_MODEL, D_MODEL)), _full((D_MODEL, D_MODEL)),
        _full((len(POOL_WINDOWS), POOL_GROUP, POOL_GROUP)), _full((1, POOL_DIM)),
    ]
    return pl.pallas_call(
        functools.partial(_ap_kernel, tq=n, seq=n, latent=False),
        grid=(batch,),
        in_specs=in_specs,
        out_specs=[pl.BlockSpec((1, n, D_MODEL), row)] + kv_specs,
        out_shape=[jax.ShapeDtypeStruct(xc.shape, F32)] + kv_shapes,
        scratch_shapes=_ap_scratch(n, 0),
        compiler_params=pltpu.CompilerParams(dimension_semantics=("arbitrary",),
                                             vmem_limit_bytes=VMEM_LIMIT),
        name="ap_ctx",
    )(sink, xc, mod_ctx, w["norm_g"], w["wq"], w["wkv"], w["wu"], w["wgz"], w["wout"],
      w["pool_w"], w["pool_scale"])


def _ctx_kv(xc, mod_ctx, w):
    batch, n, _ = xc.shape
    kv_specs, kv_shapes = _ctx_kv_specs(batch, n)
    return pl.pallas_call(
        functools.partial(_ctx_kv_kernel, n_rows=n),
        grid=(batch,),
        in_specs=[pl.BlockSpec((1, n, D_MODEL), lambda b: (b, 0, 0)),
                  _full((1, 1, 3 * D_MODEL)), _full((1, D_MODEL)), _full((D_MODEL, 2 * KV_DIM))],
        out_specs=kv_specs,
        out_shape=kv_shapes,
        scratch_shapes=[pltpu.VMEM((n, D_MODEL), BF16)],
        compiler_params=pltpu.CompilerParams(dimension_semantics=("arbitrary",),
                                             vmem_limit_bytes=VMEM_LIMIT),
        name="ctx_kv",
    )(xc, mod_ctx, w["norm_g"], w["wkv"])


def _cv_layer(x, mod, w, tq, has_halo, final_g=None):
    batch, seq, _ = x.shape
    halo = CONV_HALO
    final_norm = final_g is not None
    if has_halo:
        bpt = tq // halo
        n_blk = seq // halo
        grid = (batch, seq // tq)
        main = lambda b, i: (b, i, 0)
        x_specs = [pl.BlockSpec((1, tq, D_MODEL), main),
                   pl.BlockSpec((1, halo, D_MODEL), lambda b, i: (b, jnp.maximum(i * bpt - 1, 0), 0)),
                   pl.BlockSpec((1, halo, D_MODEL),
                                lambda b, i: (b, jnp.minimum((i + 1) * bpt, n_blk - 1), 0))]
        x_args = [x, x, x]
        mod_spec = pl.BlockSpec((1, 1, 3 * D_MODEL), lambda b, i: (b, 0, 0))
        sem = ("arbitrary", "arbitrary")
    else:
        assert tq == seq
        grid = (batch,)
        main = lambda b: (b, 0, 0)
        x_specs = [pl.BlockSpec((1, tq, D_MODEL), main)]
        x_args = [x]
        mod_spec = _full((1, 1, 3 * D_MODEL))
        sem = ("arbitrary",)
    in_specs = x_specs + [
        mod_spec, _full((1, D_MODEL)),
        _full((D_MODEL, 2 * D_MODEL)), _full((D_MODEL, D_MODEL)), _full((D_MODEL, D_MODEL)),
        _full((CONV_K, D_MODEL)), _full((1, D_MODEL)), _full((1, D_MODEL)), _full((1, D_MODEL)),
    ]
    args = x_args + [mod, w["norm_g"], w["wab"], w["wgz"], w["wout"], w["dw_w"], w["dw_b"],
                     w["ln_g"], w["ln_b"]]
    if final_norm:
        in_specs.append(_full((1, D_MODEL)))
        args.append(final_g)
    scratch = [pltpu.VMEM((tq + 2 * halo, D_MODEL), BF16),
               pltpu.VMEM((tq + 2 * halo, D_MODEL), F32),
               pltpu.VMEM((tq, D_MODEL), F32),
               pltpu.VMEM((tq, D_MODEL), F32),
               pltpu.VMEM((tq, D_MODEL), BF16)]
    return pl.pallas_call(
        functools.partial(_cv_kernel, tq=tq, seq=seq, has_halo=has_halo, final_norm=final_norm),
        grid=grid,
        in_specs=in_specs,
        out_specs=pl.BlockSpec((1, tq, D_MODEL), main),
        out_shape=jax.ShapeDtypeStruct(x.shape, F32),
        scratch_shapes=scratch,
        compiler_params=pltpu.CompilerParams(dimension_semantics=sem,
                                             vmem_limit_bytes=VMEM_LIMIT),
        name="cv_latent" if has_halo else "cv_ctx",
    )(*args)


def _rope_tables(seq):
    rows = seq // GRID_W
    row = jnp.repeat(jnp.arange(rows, dtype=F32), GRID_W)
    col = jnp.tile(jnp.arange(GRID_W, dtype=F32), rows)
    inv_freq = ROPE_THETA ** (-jnp.arange(ROPE_AXIS_PAIRS, dtype=F32) / ROPE_AXIS_PAIRS)
    ang = jnp.concatenate([row[:, None] * inv_freq, col[:, None] * inv_freq], axis=-1)
    cos, sin = jnp.cos(ang), jnp.sin(ang)
    cos_slab = jnp.tile(cos, (1, 4))
    sin_slab = jnp.concatenate([-sin, -sin, sin, sin], axis=-1)
    return cos_slab, sin_slab


def _rope_slab_perm(head_a, head_b):
    half = HEAD_DIM // 2
    cols = []
    for part in range(2):
        for head in (head_a, head_b):
            cols.extend(range(head * HEAD_DIM + part * half, head * HEAD_DIM + (part + 1) * half))
    return cols


def _q_perm():
    return np.asarray(sum((_rope_slab_perm(c, 4 + c) for c in range(4)), []), dtype=np.int32)


def _attn_out_perm():
    perm = []
    for c in range(4):
        for head in (c, 4 + c):
            perm.extend(range(head * HEAD_DIM, (head + 1) * HEAD_DIM))
    return np.asarray(perm, dtype=np.int32)


def _ap_weights(j, ap_norm_g, ap_w_in, ap_w_out, ap_pool_w, ap_pool_scale):
    q_perm = _q_perm()
    k_perm = np.asarray(_rope_slab_perm(0, 1), dtype=np.int32)
    out_perm = _attn_out_perm()
    w_in = ap_w_in[j]
    o_k = ATTN_DIM
    o_v = ATTN_DIM + KV_DIM
    o_u = ATTN_DIM + 2 * KV_DIM
    o_gz = o_u + POOL_DIM
    wq = w_in[:, :ATTN_DIM][:, q_perm] * (HEAD_DIM ** -0.5 * LOG2_E)
    wkv = jnp.concatenate([w_in[:, o_k:o_v][:, k_perm], w_in[:, o_v:o_u]], axis=1)
    wgz = w_in[:, o_gz:]
    wgz = jnp.concatenate([wgz[:, :ATTN_DIM][:, out_perm], wgz[:, ATTN_DIM:]], axis=1)
    w_out = ap_w_out[j]
    w_out = jnp.concatenate([w_out[:ATTN_DIM][out_perm], w_out[ATTN_DIM:]], axis=0)
    return {
        "norm_g": ap_norm_g[j].reshape(1, D_MODEL),
        "wq": wq.astype(BF16),
        "wkv": wkv.astype(BF16),
        "wu": w_in[:, o_u:o_gz].astype(BF16),
        "wgz": wgz.astype(BF16),
        "wout": w_out.astype(BF16),
        "pool_w": ap_pool_w[j].astype(BF16),
        "pool_scale": ap_pool_scale[j].reshape(1, POOL_DIM),
    }


def _cv_weights(j, cv_norm_g, cv_w_in, cv_w_out, cv_dw_w, cv_dw_b, cv_ln_g, cv_ln_b):
    w_in = cv_w_in[j]
    return {
        "norm_g": cv_norm_g[j].reshape(1, D_MODEL),
        "wab": w_in[:, :2 * D_MODEL].astype(BF16),
        "wgz": w_in[:, 2 * D_MODEL:].astype(BF16),
        "wout": cv_w_out[j].astype(BF16),
        "dw_w": cv_dw_w[j],
        "dw_b": cv_dw_b[j].reshape(1, D_MODEL),
        "ln_g": cv_ln_g[j].reshape(1, D_MODEL),
        "ln_b": cv_ln_b[j].reshape(1, D_MODEL),
    }


def kernel(x, c, ctx, c_ctx, ap_ada_w, ap_ada_b, ap_norm_g, ap_w_in, ap_w_out, ap_sink, ap_pool_w, ap_pool_scale, cv_ada_w, cv_ada_b, cv_norm_g, cv_w_in, cv_w_out, cv_dw_w, cv_dw_b, cv_ln_g, cv_ln_b, final_norm_g):
    batch, seq, _ = x.shape
    tq = 512
    cond = jnp.zeros((N_COND_ROWS, D_MODEL), F32).at[:batch].set(c).at[batch].set(c_ctx)
    ap_mod = _ada_all(cond, ap_ada_w, ap_ada_b).reshape(-1, N_COND_ROWS, 1, 3 * D_MODEL)
    cv_mod = _ada_all(cond, cv_ada_w, cv_ada_b).reshape(-1, N_COND_ROWS, 1, 3 * D_MODEL)
    rope = _rope_tables(seq)
    last_ap_layer = ((DEPTH - 1) // 2) * 2
    xc = ctx
    for i in range(DEPTH):
        j = i // 2
        update_ctx = i < last_ap_layer
        if i % 2 == 0:
            w = _ap_weights(j, ap_norm_g, ap_w_in, ap_w_out, ap_pool_w, ap_pool_scale)
            mod = ap_mod[j]
            mod_ctx = mod[batch:batch + 1]
            if update_ctx:
                xc, kc, vct = _ap_ctx(xc, mod_ctx, w, ap_sink[j])
            else:
                kc, vct = _ctx_kv(xc, mod_ctx, w)
            x = _ap_latent(x, mod, w, ap_sink[j], rope, kc, vct, tq)
        else:
            w = _cv_weights(j, cv_norm_g, cv_w_in, cv_w_out, cv_dw_w, cv_dw_b, cv_ln_g, cv_ln_b)
            mod = cv_mod[j]
            if update_ctx:
                xc = _cv_layer(xc, mod[batch:batch + 1], w, xc.shape[1], has_halo=False)
            final_g = final_norm_g.reshape(1, D_MODEL) if i == DEPTH - 1 else None
            x = _cv_layer(x, mod, w, tq, has_halo=True, final_g=final_g)
    return x
```

```python
import functools

import jax
import jax.numpy as jnp
import numpy as np
from jax import lax
from jax.experimental import pallas as pl
from jax.experimental.pallas import tpu as pltpu

F32 = jnp.float32
BF16 = jnp.bfloat16

D_MODEL = 1024
DEPTH = 4
GRID_W = 64
HEAD_DIM = 64
ATTN_DIM = 512
N_Q_HEADS = 8
KV_DIM = 128
WINDOW = 128
ATTN_BLOCK = 128
ROPE_THETA = 10000.0
ROPE_AXIS_PAIRS = HEAD_DIM // 4
POOL_WINDOWS = (2, 4, 8, 16)
POOL_DIM = 512
POOL_GROUP = 128
POOL_HALO = 8
CONV_K = 31
CONV_HALO = 16
AP_TILE_ROWS = 1024
CV_TILE_ROWS = 1024
CV_SUB_ROWS = 256
CONV_BLOCK = 128
CONV_PHASE_ROWS = 144
AP_IN_DIM = ATTN_DIM + 2 * KV_DIM + POOL_DIM + D_MODEL
AP_SUB_ROWS = 256
POOL_CHUNK = 64
EPS = 1e-6
NEG_INF = -1e30
LOG2_E = 1.4426950408889634

LANES = 128
SUBLANES = 8
BF16_ROWS = 16
N_COND_ROWS = 16
VMEM_LIMIT = 56 * 1024 * 1024
NT_DIMS = (((1,), (1,)), ((), ()))


def _gated_half(ha, hb):
    return ha + ha * jnp.tanh(hb)


def _silu_half(h):
    return _gated_half(h, h)


def _modnorm(x, gain, shift):
    ms = jnp.mean(x * x, axis=-1, keepdims=True)
    return (x * lax.rsqrt(ms + EPS)) * gain + shift


def _modnorm_rows(dst_ref, dst_off, src_ref, n_rows, chunk, gain, shift):
    def body(j, carry):
        r = pl.multiple_of(j * chunk, chunk)
        x = src_ref[0, pl.ds(r, chunk), :]
        dst_ref[pl.ds(pl.multiple_of(dst_off + r, BF16_ROWS), chunk), :] = (
            _modnorm(x, gain, shift).astype(BF16))
        return carry

    lax.fori_loop(0, n_rows // chunk, body, 0)


def _mod_params(mod_ref, g_ref):
    shift = mod_ref[0, :, 0:D_MODEL]
    gain = g_ref[...] * (1.0 + mod_ref[0, :, D_MODEL:2 * D_MODEL])
    gate = mod_ref[0, :, 2 * D_MODEL:3 * D_MODEL]
    return shift, gain, gate


def _rope(x, cos, sin):
    return x * cos + pltpu.roll(x, 64, 1) * sin


def _head_a_lanes(rows):
    lane = lax.broadcasted_iota(jnp.int32, (rows, LANES), 1)
    return (lane & 32) == 0


def _scores(q_stack, k_blocks, masks):
    scores = []
    for k, mask in zip(k_blocks, masks):
        s = lax.dot_general(k, q_stack, NT_DIMS, preferred_element_type=F32)
        if mask is not None:
            s = jnp.where(mask, s, NEG_INF)
        scores.append(s)
    return scores


def _softmax(scores, sink_row):
    m = sink_row
    for s in scores:
        m = jnp.maximum(m, jnp.max(s, axis=0, keepdims=True))
    denom = jnp.exp2(sink_row - m)
    probs = []
    for s in scores:
        p = jnp.exp2(s - m)
        denom = denom + jnp.sum(p, axis=0, keepdims=True)
        probs.append(p.astype(BF16))
    return probs, denom


def _sink_row(sink_ref, kv_head):
    return jnp.concatenate(
        [jnp.full((1, LANES), sink_ref[c + 4 * kv_head] * LOG2_E, F32) for c in range(4)], axis=1)


def _ap_kernel(*refs, tq, seq, latent):
    if latent:
        (sink_ref, x_ref, xp_ref, xn_ref, mod_ref, g_ref, win_ref,
         wout_ref, pw_ref, ps_ref, cos_ref, sin_ref, cosp_ref, sinp_ref, cosn_ref, sinn_ref,
         kc_ref, vct_ref, o_ref,
         h_s, q_s, kl_s, vt_s, u_s, gz_s, y_s) = refs
        halo = WINDOW
        i = pl.program_id(1)
        n_tiles = seq // tq
    else:
        (sink_ref, x_ref, mod_ref, g_ref, win_ref,
         wout_ref, pw_ref, ps_ref, o_ref, kc_out, vct_out,
         h_s, q_s, kl_s, vt_s, u_s, gz_s, y_s) = refs
        halo = 0
        i = 0
        n_tiles = 1
    t0 = i * tq
    wq_ref = win_ref.at[:, 0:ATTN_DIM]
    wkv_ref = win_ref.at[:, ATTN_DIM:ATTN_DIM + 2 * KV_DIM]
    wu_ref = win_ref.at[:, ATTN_DIM + 2 * KV_DIM:ATTN_DIM + 2 * KV_DIM + POOL_DIM]
    wgz_ref = win_ref.at[:, ATTN_DIM + 2 * KV_DIM + POOL_DIM:AP_IN_DIM]
    n_blocks = tq // ATTN_BLOCK
    halo_blocks = halo // ATTN_BLOCK
    sub = AP_SUB_ROWS
    n_sub = tq // sub
    blocks_per_sub = sub // ATTN_BLOCK

    shift, gain, gate = _mod_params(mod_ref, g_ref)
    head_a = _head_a_lanes(ATTN_BLOCK)
    dim_a = lax.broadcasted_iota(jnp.int32, (KV_DIM, ATTN_BLOCK), 0) < HEAD_DIM

    def norm_rows(dst_lo, src_ref, src_lo, n):
        h_s[dst_lo:dst_lo + n, :] = _modnorm(src_ref[0, src_lo:src_lo + n, :], gain,
                                             shift).astype(BF16)

    def project_kv(h_lo, n, c_ref, s_ref, t_lo):
        z = jnp.dot(h_s[h_lo:h_lo + n, :], wkv_ref[...], preferred_element_type=F32)
        for b in range(n // ATTN_BLOCK):
            zr = slice(b * ATTN_BLOCK, (b + 1) * ATTN_BLOCK)
            hr = slice(h_lo + b * ATTN_BLOCK, h_lo + (b + 1) * ATTN_BLOCK)
            tr = slice(t_lo + b * ATTN_BLOCK, t_lo + (b + 1) * ATTN_BLOCK)
            k = z[zr, 0:LANES]
            if latent:
                k = _rope(k, c_ref[tr, :], s_ref[tr, :])
            kl_s[0, hr, :] = jnp.where(head_a, k, 0.0).astype(BF16)
            kl_s[1, hr, :] = jnp.where(head_a, 0.0, k).astype(BF16)
            vt = z[zr, LANES:2 * LANES].T
            blk = h_lo // ATTN_BLOCK + b
            vt_s[0, blk] = jnp.where(dim_a, vt, 0.0).astype(BF16)
            vt_s[1, blk] = jnp.where(dim_a, 0.0, vt).astype(BF16)

    def project_main(sc):
        rows = slice(sc * sub, (sc + 1) * sub)
        hm = h_s[halo + sc * sub:halo + (sc + 1) * sub, :]
        u_s[POOL_HALO + sc * sub:POOL_HALO + (sc + 1) * sub, :] = jnp.dot(
            hm, wu_ref[...], preferred_element_type=F32)
        zq = jnp.dot(hm, wq_ref[...], preferred_element_type=F32)
        for c in range(4):
            sl = slice(c * LANES, (c + 1) * LANES)
            q = zq[:, sl]
            if latent:
                q = _rope(q, cos_ref[rows, :], sin_ref[rows, :])
            q_s[rows, sl] = q.astype(BF16)
        gz_s[rows, :] = jnp.dot(hm, wgz_ref[...], preferred_element_type=F32)

    def pool_halos():
        zero_halo = jnp.zeros((POOL_HALO, POOL_DIM), F32)
        if latent:
            up = jnp.dot(h_s[halo - BF16_ROWS:halo, :], wu_ref[...], preferred_element_type=F32)
            un = jnp.dot(h_s[halo + tq:halo + tq + BF16_ROWS, :], wu_ref[...],
                         preferred_element_type=F32)
            u_s[0:POOL_HALO, :] = jnp.where(i > 0, up[BF16_ROWS - POOL_HALO:BF16_ROWS], zero_halo)
            u_s[POOL_HALO + tq:2 * POOL_HALO + tq, :] = jnp.where(i < n_tiles - 1,
                                                                   un[0:POOL_HALO], zero_halo)
        else:
            u_s[0:POOL_HALO, :] = zero_halo
            u_s[POOL_HALO + tq:2 * POOL_HALO + tq, :] = zero_halo

    def pool(sc):
        win_rows = POOL_CHUNK + 2 * POOL_HALO
        diffs = []
        for r in range(sc * sub, (sc + 1) * sub, POOL_CHUNK):
            row = []
            for gi, w in enumerate(POOL_WINDOWS):
                sl = slice(gi * POOL_GROUP, (gi + 1) * POOL_GROUP)
                win = u_s[r:r + win_rows, sl]
                acc = win + pltpu.roll(win, 1, 0)
                span = 2
                while span < w:
                    acc = pltpu.roll(acc, span // 2, 0) + pltpu.roll(acc, win_rows - span // 2, 0)
                    span *= 2
                acc = acc[POOL_HALO:POOL_HALO + POOL_CHUNK]
                if 0 < r < tq - POOL_CHUNK:
                    mean = acc * (1.0 / w)
                else:
                    t = t0 + r + lax.broadcasted_iota(jnp.int32, (POOL_CHUNK, POOL_GROUP), 0)
                    cnt = (jnp.minimum(t + w // 2, seq) - jnp.maximum(t - w // 2, 0)).astype(F32)
                    mean = acc / cnt
                diff = mean - win[POOL_HALO:POOL_HALO + POOL_CHUNK]
                row.append(diff.astype(BF16))
            diffs.append(row)
        for gi in range(len(POOL_WINDOWS)):
            sl = slice(gi * POOL_GROUP, (gi + 1) * POOL_GROUP)
            d = jnp.concatenate([row[gi] for row in diffs], axis=0)
            yg = jnp.dot(d, pw_ref[gi], preferred_element_type=F32) * ps_ref[:, sl]
            y_s[sc * sub:(sc + 1) * sub, ATTN_DIM + gi * POOL_GROUP:ATTN_DIM + (gi + 1) * POOL_GROUP] = yg

    if latent:
        kk = lax.broadcasted_iota(jnp.int32, (ATTN_BLOCK, 4 * ATTN_BLOCK), 0)
        qq = lax.broadcasted_iota(jnp.int32, (ATTN_BLOCK, 4 * ATTN_BLOCK), 1) & (ATTN_BLOCK - 1)
        no_prev = jnp.where(i > 0, 0, ATTN_BLOCK)
        no_next = jnp.where(i < n_tiles - 1, 0, ATTN_BLOCK)

    out_a = lax.broadcasted_iota(jnp.int32, (KV_DIM, 4 * ATTN_BLOCK), 0) < HEAD_DIM

    def block_scores(j):
        r = j * ATTN_BLOCK
        q_stack = jnp.concatenate([q_s[r:r + ATTN_BLOCK, c * LANES:(c + 1) * LANES] for c in range(4)],
                                  axis=0)
        if latent:
            prev_ok = kk >= (qq + no_prev if j == 0 else qq)
            next_ok = kk <= (qq - no_next if j == n_blocks - 1 else qq)
            masks = [prev_ok, None, next_ok, None]
        else:
            masks = [None]
        out = []
        for kvh in range(2):
            if latent:
                k_blocks = [kl_s[kvh, r + b * ATTN_BLOCK:r + (b + 1) * ATTN_BLOCK, :] for b in range(3)]
                k_blocks.append(kc_ref[0, kvh])
            else:
                k_blocks = [kl_s[kvh]]
            out.append(_scores(q_stack, k_blocks, masks))
        return out

    def block_finish(j, scores):
        r = j * ATTN_BLOCK
        probs, vts, denoms = [], [], []
        for kvh in range(2):
            p, denom = _softmax(scores[kvh], _sink_row(sink_ref, kvh))
            probs.extend(p)
            denoms.append(denom)
            if latent:
                vts.extend([vt_s[kvh, j + b] for b in range(3)] + [vct_ref[0, kvh]])
            else:
                vts.extend([vt_s[kvh, b] for b in range(n_blocks)])
        acc = jnp.dot(jnp.concatenate(vts, axis=1), jnp.concatenate(probs, axis=0),
                      preferred_element_type=F32)
        acc = acc / jnp.where(out_a, denoms[0], denoms[1])
        for c in range(4):
            y_s[r:r + ATTN_BLOCK, c * LANES:(c + 1) * LANES] = acc[:, c * ATTN_BLOCK:(c + 1) * ATTN_BLOCK].T

    def emit(lo, n):
        rows = slice(lo, lo + n)
        yb = (y_s[rows, :] * _silu_half(gz_s[rows, :])).astype(BF16)
        out = jnp.dot(yb, wout_ref[...], preferred_element_type=F32)
        o_ref[0, rows, :] = x_ref[0, rows, :] + gate * out

    if latent:
        norm_rows(0, xp_ref, 0, halo)
    norm_rows(halo, x_ref, 0, sub)
    if latent:
        project_kv(0, halo, cosp_ref, sinp_ref, 0)
    for sc in range(n_sub):
        if sc + 1 < n_sub:
            norm_rows(halo + (sc + 1) * sub, x_ref, (sc + 1) * sub, sub)
        elif latent:
            norm_rows(halo + tq, xn_ref, 0, halo)
        project_kv(halo + sc * sub, sub, cos_ref if latent else None, sin_ref if latent else None,
                   sc * sub)
        project_main(sc)
        if sc == 0:
            assert blocks_per_sub >= 2 or n_sub == 1
            scores = block_scores(0)
    if latent:
        project_kv(halo + tq, halo, cosn_ref, sinn_ref, 0)
    pool_halos()
    if not latent:
        kc_out[0] = kl_s[...]
        for kvh in range(2):
            for b in range(n_blocks):
                vct_out[0, kvh, :, b * ATTN_BLOCK:(b + 1) * ATTN_BLOCK] = vt_s[kvh, b]

    for j in range(n_blocks):
        sc, first_of_sub = divmod(j, blocks_per_sub)
        if first_of_sub == 0:
            pool(sc)
        nxt = block_scores(j + 1) if j + 1 < n_blocks else None
        block_finish(j, scores)
        scores = nxt
        if (j + 1) % blocks_per_sub == 0:
            emit(sc * sub, sub)


def _ctx_kv_kernel(x_ref, mod_ref, g_ref, win_ref, kc_out, vct_out, h_s, *, n_rows):
    shift, gain, _ = _mod_params(mod_ref, g_ref)
    _modnorm_rows(h_s, 0, x_ref, n_rows, 128, gain, shift)
    z = jnp.dot(h_s[...], win_ref[:, ATTN_DIM:ATTN_DIM + 2 * KV_DIM], preferred_element_type=F32)
    head_a = _head_a_lanes(n_rows)
    k = z[:, 0:LANES]
    kc_out[0, 0] = jnp.where(head_a, k, 0.0).astype(BF16)
    kc_out[0, 1] = jnp.where(head_a, 0.0, k).astype(BF16)
    dim_a = lax.broadcasted_iota(jnp.int32, (KV_DIM, ATTN_BLOCK), 0) < HEAD_DIM
    for b in range(n_rows // ATTN_BLOCK):
        cols = slice(b * ATTN_BLOCK, (b + 1) * ATTN_BLOCK)
        vt = z[cols, LANES:2 * LANES].T
        vct_out[0, 0, :, cols] = jnp.where(dim_a, vt, 0.0).astype(BF16)
        vct_out[0, 1, :, cols] = jnp.where(dim_a, 0.0, vt).astype(BF16)


def _cv_kernel(*refs, tq, seq, has_halo, final_norm):
    refs = list(refs)
    x_ref = refs.pop(0)
    if has_halo:
        xp_ref = refs.pop(0)
        xn_ref = refs.pop(0)
    (mod_ref, g_ref, win_ref, wout_ref, dww_ref, dwb_ref, lng_ref, lnb_ref,
     shift_ref) = refs[:9]
    refs = refs[9:]
    wab_ref = win_ref.at[:, 0:2 * D_MODEL]
    wgz_ref = win_ref.at[:, 2 * D_MODEL:3 * D_MODEL]
    if final_norm:
        fg_ref = refs.pop(0)
    o_ref, h_s, u_s, gz_s, c_s, yb_s, zb_s = refs
    halo = CONV_HALO
    if has_halo:
        i = pl.program_id(1)
        n_tiles = seq // tq
    sub = CV_SUB_ROWS
    n_sub = tq // sub
    cb = CONV_BLOCK

    shift, gain, gate = _mod_params(mod_ref, g_ref)
    lng_h = 0.5 * lng_ref[...]
    lnb_h = 0.5 * lnb_ref[...]
    if not has_halo:
        zero_halo = jnp.zeros((halo, D_MODEL), F32)
        u_s[0:halo, :] = zero_halo
        u_s[halo + tq:2 * halo + tq, :] = zero_halo

    def project(sc):
        lo = halo + sc * sub
        hi = lo + sub
        h_s[lo:hi, :] = _modnorm(x_ref[0, sc * sub:(sc + 1) * sub, :], gain, shift).astype(BF16)
        zlo, zhi = lo, hi
        if has_halo and sc == 0:
            h_s[0:halo, :] = _modnorm(xp_ref[0], gain, shift).astype(BF16)
            zlo = 0
        if has_halo and sc == n_sub - 1:
            h_s[halo + tq:2 * halo + tq, :] = _modnorm(xn_ref[0], gain, shift).astype(BF16)
            zhi = tq + 2 * halo
        zab = jnp.dot(h_s[zlo:zhi, :], wab_ref[...], preferred_element_type=F32)
        u_s[zlo:zhi, :] = _gated_half(zab[:, 0:D_MODEL], zab[:, D_MODEL:2 * D_MODEL])
        if has_halo and sc == 0:
            u_s[0:halo, :] = jnp.where(i > 0, u_s[0:halo, :], 0.0)
        if has_halo and sc == n_sub - 1:
            u_s[halo + tq:2 * halo + tq, :] = jnp.where(i < n_tiles - 1,
                                                        u_s[halo + tq:2 * halo + tq, :], 0.0)
        gz_s[sc * sub:(sc + 1) * sub, :] = jnp.dot(h_s[lo:hi, :], wgz_ref[...],
                                                   preferred_element_type=F32)

    def mix(sc):
        for r in range(sc * sub, (sc + 1) * sub, cb):
            slot = (r // cb) % 2
            for c in range(D_MODEL // LANES):
                sl = slice(c * LANES, (c + 1) * LANES)
                for phase in range(SUBLANES):
                    n_rows = cb if phase == 0 else cb + SUBLANES
                    part = None
                    for a in range(CONV_K // SUBLANES + 1):
                        k = a * SUBLANES + phase - 1
                        if 0 <= k < CONV_K:
                            lo = r + a * SUBLANES
                            term = u_s[lo:lo + n_rows, sl] * dww_ref[k:k + 1, sl]
                            part = term if part is None else part + term
                    if phase == 0:
                        c_s[r:r + cb, sl] = part + dwb_ref[:, sl]
                    else:
                        pad = jnp.zeros((CONV_PHASE_ROWS - n_rows, LANES), F32)
                        zb_s[slot, (phase - 1) * CONV_PHASE_ROWS:phase * CONV_PHASE_ROWS, sl] = (
                            jnp.concatenate([part, pad], axis=0).astype(BF16))
            y = c_s[r:r + cb, :] + jnp.dot(shift_ref[...], zb_s[slot], preferred_element_type=F32)
            mu = jnp.mean(y, axis=-1, keepdims=True)
            yc = y - mu
            yn = yc * lax.rsqrt(jnp.mean(yc * yc, axis=-1, keepdims=True) + EPS)
            u2 = _silu_half(yn * lng_h + lnb_h)
            yb_s[r:r + cb, :] = (u2 * _silu_half(gz_s[r:r + cb, :])).astype(BF16)

    def emit(sc):
        rows = slice(sc * sub, (sc + 1) * sub)
        out = jnp.dot(yb_s[rows, :], wout_ref[...], preferred_element_type=F32)
        x_new = x_ref[0, rows, :] + gate * out
        if final_norm:
            ms = jnp.mean(x_new * x_new, axis=-1, keepdims=True)
            x_new = (x_new * lax.rsqrt(ms + EPS)) * fg_ref[...]
        o_ref[0, rows, :] = x_new

    for sc in range(min(2, n_sub)):
        project(sc)
    for sc in range(n_sub):
        if sc + 2 < n_sub:
            project(sc + 2)
        mix(sc)
        emit(sc)


def _ada_kernel(c_ref, w_ref, b_ref, o_ref):
    c = c_ref[...]
    s = c * jax.nn.sigmoid(c)
    o_ref[0] = jnp.dot(s, w_ref[0], precision=lax.Precision.HIGHEST,
                       preferred_element_type=F32) + b_ref[0]


def _ada_all(cond, ada_w, ada_b):
    n_layers = ada_w.shape[0]
    tn = 768
    return pl.pallas_call(
        _ada_kernel,
        grid=(n_layers, 3 * D_MODEL // tn),
        in_specs=[pl.BlockSpec((N_COND_ROWS, D_MODEL), lambda l, n: (0, 0)),
                  pl.BlockSpec((1, D_MODEL, tn), lambda l, n: (l, 0, n)),
                  pl.BlockSpec((1, 1, tn), lambda l, n: (l, 0, n))],
        out_specs=pl.BlockSpec((1, N_COND_ROWS, tn), lambda l, n: (l, 0, n)),
        out_shape=jax.ShapeDtypeStruct((n_layers, N_COND_ROWS, 3 * D_MODEL), F32),
        compiler_params=pltpu.CompilerParams(dimension_semantics=("arbitrary", "arbitrary"),
                                             vmem_limit_bytes=VMEM_LIMIT),
        name="adaln",
    )(cond, ada_w, ada_b.reshape(n_layers, 1, 3 * D_MODEL))


def _full(shape):
    return pl.BlockSpec(shape, lambda *_: (0,) * len(shape), pipeline_mode=pl.Buffered(1))


def _ap_scratch(tq, halo):
    n_kv = tq + 2 * halo
    return [pltpu.VMEM((n_kv, D_MODEL), BF16),
            pltpu.VMEM((tq, ATTN_DIM), BF16),
            pltpu.VMEM((2, n_kv, KV_DIM), BF16),
            pltpu.VMEM((2, n_kv // ATTN_BLOCK, KV_DIM, ATTN_BLOCK), BF16),
            pltpu.VMEM((tq + 2 * POOL_HALO, POOL_DIM), F32),
            pltpu.VMEM((tq, D_MODEL), F32),
            pltpu.VMEM((tq, D_MODEL), F32)]


def _ap_latent(x, mod, w, sink, rope, kc, vct, tq):
    batch, seq, _ = x.shape
    bpt = tq // WINDOW
    n_blk = seq // WINDOW
    n_ctx = kc.shape[2]
    main = lambda b, i: (b, i, 0)
    prev = lambda b, i: (b, jnp.maximum(i * bpt - 1, 0), 0)
    nxt = lambda b, i: (b, jnp.minimum((i + 1) * bpt, n_blk - 1), 0)
    tmain = lambda b, i: (i, 0)
    tprev = lambda b, i: (jnp.maximum(i * bpt - 1, 0), 0)
    tnxt = lambda b, i: (jnp.minimum((i + 1) * bpt, n_blk - 1), 0)
    cos, sin = rope
    in_specs = [
        pl.BlockSpec(memory_space=pltpu.SMEM),
        pl.BlockSpec((1, tq, D_MODEL), main),
        pl.BlockSpec((1, WINDOW, D_MODEL), prev),
        pl.BlockSpec((1, WINDOW, D_MODEL), nxt),
        pl.BlockSpec((1, 1, 3 * D_MODEL), lambda b, i: (b, 0, 0)),
        _full((1, D_MODEL)),
        _full((D_MODEL, AP_IN_DIM)), _full((D_MODEL, D_MODEL)),
        _full((len(POOL_WINDOWS), POOL_GROUP, POOL_GROUP)), _full((1, POOL_DIM)),
        pl.BlockSpec((tq, LANES), tmain), pl.BlockSpec((tq, LANES), tmain),
        pl.BlockSpec((WINDOW, LANES), tprev), pl.BlockSpec((WINDOW, LANES), tprev),
        pl.BlockSpec((WINDOW, LANES), tnxt), pl.BlockSpec((WINDOW, LANES), tnxt),
        pl.BlockSpec((1, 2, n_ctx, KV_DIM), lambda b, i: (b, 0, 0, 0)),
        pl.BlockSpec((1, 2, KV_DIM, n_ctx), lambda b, i: (b, 0, 0, 0)),
    ]
    return pl.pallas_call(
        functools.partial(_ap_kernel, tq=tq, seq=seq, latent=True),
        grid=(batch, seq // tq),
        in_specs=in_specs,
        out_specs=pl.BlockSpec((1, tq, D_MODEL), main),
        out_shape=jax.ShapeDtypeStruct(x.shape, F32),
        scratch_shapes=_ap_scratch(tq, WINDOW),
        compiler_params=pltpu.CompilerParams(dimension_semantics=("arbitrary", "arbitrary"),
                                             vmem_limit_bytes=VMEM_LIMIT),
        name="ap_latent",
    )(sink, x, x, x, mod, w["norm_g"], w["win"], w["wout"],
      w["pool_w"], w["pool_scale"], cos, sin, cos, sin, cos, sin, kc, vct)


def _ctx_kv_specs(batch, n):
    kc_spec = pl.BlockSpec((1, 2, n, KV_DIM), lambda b: (b, 0, 0, 0))
    vct_spec = pl.BlockSpec((1, 2, KV_DIM, n), lambda b: (b, 0, 0, 0))
    kc_shape = jax.ShapeDtypeStruct((batch, 2, n, KV_DIM), BF16)
    vct_shape = jax.ShapeDtypeStruct((batch, 2, KV_DIM, n), BF16)
    return [kc_spec, vct_spec], [kc_shape, vct_shape]


def _ap_ctx(xc, mod_ctx, w, sink):
    batch, n, _ = xc.shape
    row = lambda b: (b, 0, 0)
    kv_specs, kv_shapes = _ctx_kv_specs(batch, n)
    in_specs = [
        pl.BlockSpec(memory_space=pltpu.SMEM),
        pl.BlockSpec((1, n, D_MODEL), row),
        _full((1, 1, 3 * D_MODEL)),
        _full((1, D_MODEL)),
        _full((D_MODEL, AP_IN_DIM)), _full((D_MODEL, D_MODEL)),
        _full((len(POOL_WINDOWS), POOL_GROUP, POOL_GROUP)), _full((1, POOL_DIM)),
    ]
    return pl.pallas_call(
        functools.partial(_ap_kernel, tq=n, seq=n, latent=False),
        grid=(batch,),
        in_specs=in_specs,
        out_specs=[pl.BlockSpec((1, n, D_MODEL), row)] + kv_specs,
        out_shape=[jax.ShapeDtypeStruct(xc.shape, F32)] + kv_shapes,
        scratch_shapes=_ap_scratch(n, 0),
        compiler_params=pltpu.CompilerParams(dimension_semantics=("arbitrary",),
                                             vmem_limit_bytes=VMEM_LIMIT),
        name="ap_ctx",
    )(sink, xc, mod_ctx, w["norm_g"], w["win"], w["wout"],
      w["pool_w"], w["pool_scale"])


def _ctx_kv(xc, mod_ctx, w):
    batch, n, _ = xc.shape
    kv_specs, kv_shapes = _ctx_kv_specs(batch, n)
    return pl.pallas_call(
        functools.partial(_ctx_kv_kernel, n_rows=n),
        grid=(batch,),
        in_specs=[pl.BlockSpec((1, n, D_MODEL), lambda b: (b, 0, 0)),
                  _full((1, 1, 3 * D_MODEL)), _full((1, D_MODEL)), _full((D_MODEL, AP_IN_DIM))],
        out_specs=kv_specs,
        out_shape=kv_shapes,
        scratch_shapes=[pltpu.VMEM((n, D_MODEL), BF16)],
        compiler_params=pltpu.CompilerParams(dimension_semantics=("arbitrary",),
                                             vmem_limit_bytes=VMEM_LIMIT),
        name="ctx_kv",
    )(xc, mod_ctx, w["norm_g"], w["win"])


def _cv_layer(x, mod, w, tq, has_halo, final_g=None):
    batch, seq, _ = x.shape
    halo = CONV_HALO
    final_norm = final_g is not None
    if has_halo:
        bpt = tq // halo
        n_blk = seq // halo
        grid = (batch, seq // tq)
        main = lambda b, i: (b, i, 0)
        x_specs = [pl.BlockSpec((1, tq, D_MODEL), main),
                   pl.BlockSpec((1, halo, D_MODEL), lambda b, i: (b, jnp.maximum(i * bpt - 1, 0), 0)),
                   pl.BlockSpec((1, halo, D_MODEL),
                                lambda b, i: (b, jnp.minimum((i + 1) * bpt, n_blk - 1), 0))]
        x_args = [x, x, x]
        mod_spec = pl.BlockSpec((1, 1, 3 * D_MODEL), lambda b, i: (b, 0, 0))
        sem = ("arbitrary", "arbitrary")
    else:
        assert tq == seq
        grid = (batch,)
        main = lambda b: (b, 0, 0)
        x_specs = [pl.BlockSpec((1, tq, D_MODEL), main)]
        x_args = [x]
        mod_spec = _full((1, 1, 3 * D_MODEL))
        sem = ("arbitrary",)
    in_specs = x_specs + [
        mod_spec, _full((1, D_MODEL)),
        _full((D_MODEL, 3 * D_MODEL)), _full((D_MODEL, D_MODEL)),
        _full((CONV_K, D_MODEL)), _full((1, D_MODEL)), _full((1, D_MODEL)), _full((1, D_MODEL)),
        _full((CONV_BLOCK, (SUBLANES - 1) * CONV_PHASE_ROWS)),
    ]
    args = x_args + [mod, w["norm_g"], w["win"], w["wout"], w["dw_w"], w["dw_b"],
                     w["ln_g"], w["ln_b"], _conv_shift_matrix()]
    if final_norm:
        in_specs.append(_full((1, D_MODEL)))
        args.append(final_g)
    scratch = [pltpu.VMEM((tq + 2 * halo, D_MODEL), BF16),
               pltpu.VMEM((tq + 2 * halo, D_MODEL), F32),
               pltpu.VMEM((tq, D_MODEL), F32),
               pltpu.VMEM((tq, D_MODEL), F32),
               pltpu.VMEM((tq, D_MODEL), BF16),
               pltpu.VMEM((2, (SUBLANES - 1) * CONV_PHASE_ROWS, D_MODEL), BF16)]
    return pl.pallas_call(
        functools.partial(_cv_kernel, tq=tq, seq=seq, has_halo=has_halo, final_norm=final_norm),
        grid=grid,
        in_specs=in_specs,
        out_specs=pl.BlockSpec((1, tq, D_MODEL), main),
        out_shape=jax.ShapeDtypeStruct(x.shape, F32),
        scratch_shapes=scratch,
        compiler_params=pltpu.CompilerParams(dimension_semantics=sem,
                                             vmem_limit_bytes=VMEM_LIMIT),
        name="cv_latent" if has_halo else "cv_ctx",
    )(*args)


def _conv_shift_matrix():
    m = np.zeros((CONV_BLOCK, (SUBLANES - 1) * CONV_PHASE_ROWS), np.float32)
    j = np.arange(CONV_BLOCK)
    for p in range(1, SUBLANES):
        m[j, (p - 1) * CONV_PHASE_ROWS + j + p] = 1.0
    return jnp.asarray(m, dtype=BF16)


def _rope_tables(seq):
    rows = seq // GRID_W
    row = jnp.repeat(jnp.arange(rows, dtype=F32), GRID_W)
    col = jnp.tile(jnp.arange(GRID_W, dtype=F32), rows)
    inv_freq = ROPE_THETA ** (-jnp.arange(ROPE_AXIS_PAIRS, dtype=F32) / ROPE_AXIS_PAIRS)
    ang = jnp.concatenate([row[:, None] * inv_freq, col[:, None] * inv_freq], axis=-1)
    cos, sin = jnp.cos(ang), jnp.sin(ang)
    cos_slab = jnp.tile(cos, (1, 4))
    sin_slab = jnp.concatenate([-sin, -sin, sin, sin], axis=-1)
    return cos_slab, sin_slab


def _rope_slab_perm(head_a, head_b):
    half = HEAD_DIM // 2
    cols = []
    for part in range(2):
        for head in (head_a, head_b):
            cols.extend(range(head * HEAD_DIM + part * half, head * HEAD_DIM + (part + 1) * half))
    return cols


def _q_perm():
    return np.asarray(sum((_rope_slab_perm(c, 4 + c) for c in range(4)), []), dtype=np.int32)


def _attn_out_perm():
    perm = []
    for c in range(4):
        for head in (c, 4 + c):
            perm.extend(range(head * HEAD_DIM, (head + 1) * HEAD_DIM))
    return np.asarray(perm, dtype=np.int32)


def _ap_weights(j, ap_norm_g, ap_w_in, ap_w_out, ap_pool_w, ap_pool_scale):
    o_k = ATTN_DIM
    o_v = ATTN_DIM + KV_DIM
    o_gz = ATTN_DIM + 2 * KV_DIM + POOL_DIM
    cols = np.arange(AP_IN_DIM, dtype=np.int32)
    cols[:o_k] = _q_perm()
    cols[o_k:o_v] = o_k + np.asarray(_rope_slab_perm(0, 1), dtype=np.int32)
    cols[o_gz:o_gz + ATTN_DIM] = o_gz + _attn_out_perm()
    scale = np.ones((AP_IN_DIM,), np.float32)
    scale[:o_k] = HEAD_DIM ** -0.5 * LOG2_E
    scale[o_gz:] = 0.5
    w_in = (ap_w_in[j][:, cols] * scale).astype(BF16)
    rows = np.arange(D_MODEL, dtype=np.int32)
    rows[:ATTN_DIM] = _attn_out_perm()
    return {
        "norm_g": ap_norm_g[j].reshape(1, D_MODEL),
        "win": w_in,
        "wout": ap_w_out[j][rows].astype(BF16),
        "pool_w": ap_pool_w[j].astype(BF16),
        "pool_scale": ap_pool_scale[j].reshape(1, POOL_DIM),
    }


def _cv_weights(j, cv_norm_g, cv_w_in, cv_w_out, cv_dw_w, cv_dw_b, cv_ln_g, cv_ln_b):
    return {
        "norm_g": cv_norm_g[j].reshape(1, D_MODEL),
        "win": (0.5 * cv_w_in[j]).astype(BF16),
        "wout": cv_w_out[j].astype(BF16),
        "dw_w": cv_dw_w[j],
        "dw_b": cv_dw_b[j].reshape(1, D_MODEL),
        "ln_g": cv_ln_g[j].reshape(1, D_MODEL),
        "ln_b": cv_ln_b[j].reshape(1, D_MODEL),
    }


def kernel(x, c, ctx, c_ctx, ap_ada_w, ap_ada_b, ap_norm_g, ap_w_in, ap_w_out, ap_sink, ap_pool_w, ap_pool_scale, cv_ada_w, cv_ada_b, cv_norm_g, cv_w_in, cv_w_out, cv_dw_w, cv_dw_b, cv_ln_g, cv_ln_b, final_norm_g):
    batch, seq, _ = x.shape
    cond = jnp.zeros((N_COND_ROWS, D_MODEL), F32).at[:batch].set(c).at[batch].set(c_ctx)
    ap_mod = _ada_all(cond, ap_ada_w, ap_ada_b).reshape(-1, N_COND_ROWS, 1, 3 * D_MODEL)
    cv_mod = _ada_all(cond, cv_ada_w, cv_ada_b).reshape(-1, N_COND_ROWS, 1, 3 * D_MODEL)
    rope = _rope_tables(seq)
    last_ap_layer = ((DEPTH - 1) // 2) * 2
    xc = ctx
    for i in range(DEPTH):
        j = i // 2
        update_ctx = i < last_ap_layer
        if i % 2 == 0:
            w = _ap_weights(j, ap_norm_g, ap_w_in, ap_w_out, ap_pool_w, ap_pool_scale)
            mod = ap_mod[j]
            mod_ctx = mod[batch:batch + 1]
            if update_ctx:
                xc, kc, vct = _ap_ctx(xc, mod_ctx, w, ap_sink[j])
            else:
                kc, vct = _ctx_kv(xc, mod_ctx, w)
            x = _ap_latent(x, mod, w, ap_sink[j], rope, kc, vct, AP_TILE_ROWS)
        else:
            w = _cv_weights(j, cv_norm_g, cv_w_in, cv_w_out, cv_dw_w, cv_dw_b, cv_ln_g, cv_ln_b)
            mod = cv_mod[j]
            if update_ctx:
                xc = _cv_layer(xc, mod[batch:batch + 1], w, xc.shape[1], has_halo=False)
            final_g = final_norm_g.reshape(1, D_MODEL) if i == DEPTH - 1 else None
            x = _cv_layer(x, mod, w, CV_TILE_ROWS, has_halo=True, final_g=final_g)
    return x
```

```python
import functools

import jax
import jax.numpy as jnp
import numpy as np
from jax import lax
from jax.experimental import pallas as pl
from jax.experimental.pallas import tpu as pltpu

F32 = jnp.float32
BF16 = jnp.bfloat16

D_MODEL = 1024
DEPTH = 4
GRID_W = 64
HEAD_DIM = 64
ATTN_DIM = 512
N_Q_HEADS = 8
KV_DIM = 128
WINDOW = 128
ATTN_BLOCK = 128
ROPE_THETA = 10000.0
ROPE_AXIS_PAIRS = HEAD_DIM // 4
POOL_WINDOWS = (2, 4, 8, 16)
POOL_DIM = 512
POOL_GROUP = 128
POOL_HALO = 8
CONV_K = 31
CONV_HALO = 16
AP_TILE_ROWS = 1024
CV_TILE_ROWS = 1024
CV_SUB_ROWS = 256
CONV_BLOCK = 128
CONV_PHASE_ROWS = 144
AP_IN_DIM = ATTN_DIM + 2 * KV_DIM + POOL_DIM + D_MODEL
AP_SUB_ROWS = 256
POOL_CHUNK = 64
EPS = 1e-6
NEG_INF = -1e30
LOG2_E = 1.4426950408889634

LANES = 128
SUBLANES = 8
BF16_ROWS = 16
N_COND_ROWS = 16
VMEM_LIMIT = 56 * 1024 * 1024
NT_DIMS = (((1,), (1,)), ((), ()))


def _gated_half(ha, hb):
    return ha + ha * jnp.tanh(hb)


def _silu_half(h):
    return _gated_half(h, h)


def _modnorm(x, gain, shift):
    ms = jnp.mean(x * x, axis=-1, keepdims=True)
    return (x * lax.rsqrt(ms + EPS)) * gain + shift


def _modnorm_rows(dst_ref, dst_off, src_ref, n_rows, chunk, gain, shift):
    def body(j, carry):
        r = pl.multiple_of(j * chunk, chunk)
        x = src_ref[0, pl.ds(r, chunk), :]
        dst_ref[pl.ds(pl.multiple_of(dst_off + r, BF16_ROWS), chunk), :] = (
            _modnorm(x, gain, shift).astype(BF16))
        return carry

    lax.fori_loop(0, n_rows // chunk, body, 0)


def _mod_params(mod_ref, g_ref):
    shift = mod_ref[0, :, 0:D_MODEL]
    gain = g_ref[...] * (1.0 + mod_ref[0, :, D_MODEL:2 * D_MODEL])
    gate = mod_ref[0, :, 2 * D_MODEL:3 * D_MODEL]
    return shift, gain, gate


def _rope(x, cos, sin):
    return x * cos + pltpu.roll(x, 64, 1) * sin


def _head_a_lanes(rows):
    lane = lax.broadcasted_iota(jnp.int32, (rows, LANES), 1)
    return (lane & 32) == 0


def _scores(q_stack, k_blocks, masks):
    scores = []
    for k, mask in zip(k_blocks, masks):
        s = lax.dot_general(k, q_stack, NT_DIMS, preferred_element_type=F32)
        if mask is not None:
            s = jnp.where(mask, s, NEG_INF)
        scores.append(s)
    return scores


def _softmax(scores, sink_row):
    m = sink_row
    for s in scores:
        m = jnp.maximum(m, jnp.max(s, axis=0, keepdims=True))
    denom = jnp.exp2(sink_row - m)
    probs = []
    for s in scores:
        p = jnp.exp2(s - m)
        denom = denom + jnp.sum(p, axis=0, keepdims=True)
        probs.append(p.astype(BF16))
    return probs, denom


def _sink_row(sink_ref, kv_head):
    return jnp.concatenate(
        [jnp.full((1, LANES), sink_ref[c + 4 * kv_head] * LOG2_E, F32) for c in range(4)], axis=1)


def _ap_kernel(*refs, tq, seq, latent):
    if latent:
        (sink_ref, x_ref, xp_ref, xn_ref, mod_ref, g_ref, win_ref,
         wout_ref, pw_ref, ps_ref, cos_ref, sin_ref, cosp_ref, sinp_ref, cosn_ref, sinn_ref,
         kc_ref, vct_ref, o_ref,
         h_s, q_s, kl_s, vt_s, u_s, gz_s, y_s) = refs
        halo = WINDOW
        i = pl.program_id(1)
        n_tiles = seq // tq
    else:
        (sink_ref, x_ref, mod_ref, g_ref, win_ref,
         wout_ref, pw_ref, ps_ref, o_ref, kc_out, vct_out,
         h_s, q_s, kl_s, vt_s, u_s, gz_s, y_s) = refs
        halo = 0
        i = 0
        n_tiles = 1
    t0 = i * tq
    wq_ref = win_ref.at[:, 0:ATTN_DIM]
    wkv_ref = win_ref.at[:, ATTN_DIM:ATTN_DIM + 2 * KV_DIM]
    wu_ref = win_ref.at[:, ATTN_DIM + 2 * KV_DIM:ATTN_DIM + 2 * KV_DIM + POOL_DIM]
    wgz_ref = win_ref.at[:, ATTN_DIM + 2 * KV_DIM + POOL_DIM:AP_IN_DIM]
    n_blocks = tq // ATTN_BLOCK
    halo_blocks = halo // ATTN_BLOCK
    sub = AP_SUB_ROWS
    n_sub = tq // sub
    blocks_per_sub = sub // ATTN_BLOCK

    shift, gain, gate = _mod_params(mod_ref, g_ref)
    head_a = _head_a_lanes(ATTN_BLOCK)
    dim_a = lax.broadcasted_iota(jnp.int32, (KV_DIM, ATTN_BLOCK), 0) < HEAD_DIM

    def norm_rows(dst_lo, src_ref, src_lo, n):
        h_s[dst_lo:dst_lo + n, :] = _modnorm(src_ref[0, src_lo:src_lo + n, :], gain,
                                             shift).astype(BF16)

    def project_kv(h_lo, n, c_ref, s_ref, t_lo):
        z = jnp.dot(h_s[h_lo:h_lo + n, :], wkv_ref[...], preferred_element_type=F32)
        for b in range(n // ATTN_BLOCK):
            zr = slice(b * ATTN_BLOCK, (b + 1) * ATTN_BLOCK)
            hr = slice(h_lo + b * ATTN_BLOCK, h_lo + (b + 1) * ATTN_BLOCK)
            tr = slice(t_lo + b * ATTN_BLOCK, t_lo + (b + 1) * ATTN_BLOCK)
            k = z[zr, 0:LANES]
            if latent:
                k = _rope(k, c_ref[tr, :], s_ref[tr, :])
            kl_s[0, hr, :] = jnp.where(head_a, k, 0.0).astype(BF16)
            kl_s[1, hr, :] = jnp.where(head_a, 0.0, k).astype(BF16)
            vt = z[zr, LANES:2 * LANES].T
            blk = h_lo // ATTN_BLOCK + b
            vt_s[0, blk] = jnp.where(dim_a, vt, 0.0).astype(BF16)
            vt_s[1, blk] = jnp.where(dim_a, 0.0, vt).astype(BF16)

    def project_main(sc):
        rows = slice(sc * sub, (sc + 1) * sub)
        hm = h_s[halo + sc * sub:halo + (sc + 1) * sub, :]
        u_s[POOL_HALO + sc * sub:POOL_HALO + (sc + 1) * sub, :] = jnp.dot(
            hm, wu_ref[...], preferred_element_type=F32)
        zq = jnp.dot(hm, wq_ref[...], preferred_element_type=F32)
        for c in range(4):
            sl = slice(c * LANES, (c + 1) * LANES)
            q = zq[:, sl]
            if latent:
                q = _rope(q, cos_ref[rows, :], sin_ref[rows, :])
            q_s[rows, sl] = q.astype(BF16)
        gz_s[rows, :] = jnp.dot(hm, wgz_ref[...], preferred_element_type=F32)

    def pool_halos():
        zero_halo = jnp.zeros((POOL_HALO, POOL_DIM), F32)
        if latent:
            up = jnp.dot(h_s[halo - BF16_ROWS:halo, :], wu_ref[...], preferred_element_type=F32)
            un = jnp.dot(h_s[halo + tq:halo + tq + BF16_ROWS, :], wu_ref[...],
                         preferred_element_type=F32)
            u_s[0:POOL_HALO, :] = jnp.where(i > 0, up[BF16_ROWS - POOL_HALO:BF16_ROWS], zero_halo)
            u_s[POOL_HALO + tq:2 * POOL_HALO + tq, :] = jnp.where(i < n_tiles - 1,
                                                                   un[0:POOL_HALO], zero_halo)
        else:
            u_s[0:POOL_HALO, :] = zero_halo
            u_s[POOL_HALO + tq:2 * POOL_HALO + tq, :] = zero_halo

    def pool(sc):
        win_rows = POOL_CHUNK + 2 * POOL_HALO
        diffs = []
        for r in range(sc * sub, (sc + 1) * sub, POOL_CHUNK):
            row = []
            for gi, w in enumerate(POOL_WINDOWS):
                sl = slice(gi * POOL_GROUP, (gi + 1) * POOL_GROUP)
                win = u_s[r:r + win_rows, sl]
                acc = win + pltpu.roll(win, 1, 0)
                span = 2
                while span < w:
                    acc = pltpu.roll(acc, span // 2, 0) + pltpu.roll(acc, win_rows - span // 2, 0)
                    span *= 2
                acc = acc[POOL_HALO:POOL_HALO + POOL_CHUNK]
                if 0 < r < tq - POOL_CHUNK:
                    mean = acc * (1.0 / w)
                else:
                    t = t0 + r + lax.broadcasted_iota(jnp.int32, (POOL_CHUNK, POOL_GROUP), 0)
                    cnt = (jnp.minimum(t + w // 2, seq) - jnp.maximum(t - w // 2, 0)).astype(F32)
                    mean = acc / cnt
                diff = mean - win[POOL_HALO:POOL_HALO + POOL_CHUNK]
                row.append(diff.astype(BF16))
            diffs.append(row)
        for gi in range(len(POOL_WINDOWS)):
            sl = slice(gi * POOL_GROUP, (gi + 1) * POOL_GROUP)
            d = jnp.concatenate([row[gi] for row in diffs], axis=0)
            yg = jnp.dot(d, pw_ref[gi], preferred_element_type=F32) * ps_ref[:, sl]
            y_s[sc * sub:(sc + 1) * sub, ATTN_DIM + gi * POOL_GROUP:ATTN_DIM + (gi + 1) * POOL_GROUP] = yg

    if latent:
        kk = lax.broadcasted_iota(jnp.int32, (ATTN_BLOCK, 4 * ATTN_BLOCK), 0)
        qq = lax.broadcasted_iota(jnp.int32, (ATTN_BLOCK, 4 * ATTN_BLOCK), 1) & (ATTN_BLOCK - 1)
        no_prev = jnp.where(i > 0, 0, ATTN_BLOCK)
        no_next = jnp.where(i < n_tiles - 1, 0, ATTN_BLOCK)

    out_a = lax.broadcasted_iota(jnp.int32, (KV_DIM, 4 * ATTN_BLOCK), 0) < HEAD_DIM

    def block_scores(j):
        r = j * ATTN_BLOCK
        q_stack = jnp.concatenate([q_s[r:r + ATTN_BLOCK, c * LANES:(c + 1) * LANES] for c in range(4)],
                                  axis=0)
        if latent:
            prev_ok = kk >= (qq + no_prev if j == 0 else qq)
            next_ok = kk <= (qq - no_next if j == n_blocks - 1 else qq)
            masks = [prev_ok, None, next_ok, None]
        else:
            masks = [None]
        out = []
        for kvh in range(2):
            if latent:
                k_blocks = [kl_s[kvh, r + b * ATTN_BLOCK:r + (b + 1) * ATTN_BLOCK, :] for b in range(3)]
                k_blocks.append(kc_ref[0, kvh])
            else:
                k_blocks = [kl_s[kvh]]
            out.append(_scores(q_stack, k_blocks, masks))
        return out

    def block_finish(j, scores):
        r = j * ATTN_BLOCK
        probs, vts, denoms = [], [], []
        for kvh in range(2):
            p, denom = _softmax(scores[kvh], _sink_row(sink_ref, kvh))
            probs.extend(p)
            denoms.append(denom)
            if latent:
                vts.extend([vt_s[kvh, j + b] for b in range(3)] + [vct_ref[0, kvh]])
            else:
                vts.extend([vt_s[kvh, b] for b in range(n_blocks)])
        acc = jnp.dot(jnp.concatenate(vts, axis=1), jnp.concatenate(probs, axis=0),
                      preferred_element_type=F32)
        acc = acc / jnp.where(out_a, denoms[0], denoms[1])
        for c in range(4):
            y_s[r:r + ATTN_BLOCK, c * LANES:(c + 1) * LANES] = acc[:, c * ATTN_BLOCK:(c + 1) * ATTN_BLOCK].T

    def emit(lo, n):
        rows = slice(lo, lo + n)
        yb = (y_s[rows, :] * _silu_half(gz_s[rows, :])).astype(BF16)
        out = jnp.dot(yb, wout_ref[...], preferred_element_type=F32)
        o_ref[0, rows, :] = x_ref[0, rows, :] + gate * out

    if latent:
        norm_rows(0, xp_ref, 0, halo)
    norm_rows(halo, x_ref, 0, sub)
    if latent:
        project_kv(0, halo, cosp_ref, sinp_ref, 0)
    for sc in range(n_sub):
        if sc + 1 < n_sub:
            norm_rows(halo + (sc + 1) * sub, x_ref, (sc + 1) * sub, sub)
        elif latent:
            norm_rows(halo + tq, xn_ref, 0, halo)
        project_kv(halo + sc * sub, sub, cos_ref if latent else None, sin_ref if latent else None,
                   sc * sub)
        project_main(sc)
        if sc == 0:
            assert blocks_per_sub >= 2 or n_sub == 1
            scores = block_scores(0)
    if latent:
        project_kv(halo + tq, halo, cosn_ref, sinn_ref, 0)
    pool_halos()
    if not latent:
        kc_out[0] = kl_s[...]
        for kvh in range(2):
            for b in range(n_blocks):
                vct_out[0, kvh, :, b * ATTN_BLOCK:(b + 1) * ATTN_BLOCK] = vt_s[kvh, b]

    for j in range(n_blocks):
        sc, first_of_sub = divmod(j, blocks_per_sub)
        if first_of_sub == 0:
            pool(sc)
        nxt = block_scores(j + 1) if j + 1 < n_blocks else None
        block_finish(j, scores)
        scores = nxt
        if (j + 1) % blocks_per_sub == 0:
            emit(sc * sub, sub)


def _ctx_kv_kernel(x_ref, mod_ref, g_ref, win_ref, kc_out, vct_out, h_s, *, n_rows):
    shift, gain, _ = _mod_params(mod_ref, g_ref)
    _modnorm_rows(h_s, 0, x_ref, n_rows, 128, gain, shift)
    z = jnp.dot(h_s[...], win_ref[:, ATTN_DIM:ATTN_DIM + 2 * KV_DIM], preferred_element_type=F32)
    head_a = _head_a_lanes(n_rows)
    k = z[:, 0:LANES]
    kc_out[0, 0] = jnp.where(head_a, k, 0.0).astype(BF16)
    kc_out[0, 1] = jnp.where(head_a, 0.0, k).astype(BF16)
    dim_a = lax.broadcasted_iota(jnp.int32, (KV_DIM, ATTN_BLOCK), 0) < HEAD_DIM
    for b in range(n_rows // ATTN_BLOCK):
        cols = slice(b * ATTN_BLOCK, (b + 1) * ATTN_BLOCK)
        vt = z[cols, LANES:2 * LANES].T
        vct_out[0, 0, :, cols] = jnp.where(dim_a, vt, 0.0).astype(BF16)
        vct_out[0, 1, :, cols] = jnp.where(dim_a, 0.0, vt).astype(BF16)


def _cv_kernel(*refs, tq, seq, has_halo, final_norm):
    refs = list(refs)
    x_ref = refs.pop(0)
    if has_halo:
        xp_ref = refs.pop(0)
        xn_ref = refs.pop(0)
    (mod_ref, g_ref, win_ref, wout_ref, dww_ref, dwb_ref, lng_ref, lnb_ref,
     shift_ref) = refs[:9]
    refs = refs[9:]
    wab_ref = win_ref.at[:, 0:2 * D_MODEL]
    wgz_ref = win_ref.at[:, 2 * D_MODEL:3 * D_MODEL]
    if final_norm:
        fg_ref = refs.pop(0)
    o_ref, h_s, u_s, gz_s, c_s, yb_s, zb_s = refs
    halo = CONV_HALO
    if has_halo:
        i = pl.program_id(1)
        n_tiles = seq // tq
    sub = CV_SUB_ROWS
    n_sub = tq // sub
    cb = CONV_BLOCK

    shift, gain, gate = _mod_params(mod_ref, g_ref)
    lng_h = 0.5 * lng_ref[...]
    lnb_h = 0.5 * lnb_ref[...]
    if not has_halo:
        zero_halo = jnp.zeros((halo, D_MODEL), F32)
        u_s[0:halo, :] = zero_halo
        u_s[halo + tq:2 * halo + tq, :] = zero_halo

    def project(sc):
        lo = halo + sc * sub
        hi = lo + sub
        h_s[lo:hi, :] = _modnorm(x_ref[0, sc * sub:(sc + 1) * sub, :], gain, shift).astype(BF16)
        zlo, zhi = lo, hi
        if has_halo and sc == 0:
            h_s[0:halo, :] = _modnorm(xp_ref[0], gain, shift).astype(BF16)
            zlo = 0
        if has_halo and sc == n_sub - 1:
            h_s[halo + tq:2 * halo + tq, :] = _modnorm(xn_ref[0], gain, shift).astype(BF16)
            zhi = tq + 2 * halo
        zab = jnp.dot(h_s[zlo:zhi, :], wab_ref[...], preferred_element_type=F32)
        u_s[zlo:zhi, :] = _gated_half(zab[:, 0:D_MODEL], zab[:, D_MODEL:2 * D_MODEL])
        if has_halo and sc == 0:
            u_s[0:halo, :] = jnp.where(i > 0, u_s[0:halo, :], 0.0)
        if has_halo and sc == n_sub - 1:
            u_s[halo + tq:2 * halo + tq, :] = jnp.where(i < n_tiles - 1,
                                                        u_s[halo + tq:2 * halo + tq, :], 0.0)
        gz_s[sc * sub:(sc + 1) * sub, :] = jnp.dot(h_s[lo:hi, :], wgz_ref[...],
                                                   preferred_element_type=F32)

    def mix(sc):
        for r in range(sc * sub, (sc + 1) * sub, cb):
            slot = (r // cb) % 2
            for c in range(D_MODEL // LANES):
                sl = slice(c * LANES, (c + 1) * LANES)
                for phase in range(SUBLANES):
                    n_rows = cb if phase == 0 else cb + SUBLANES
                    part = None
                    for a in range(CONV_K // SUBLANES + 1):
                        k = a * SUBLANES + phase - 1
                        if 0 <= k < CONV_K:
                            lo = r + a * SUBLANES
                            term = u_s[lo:lo + n_rows, sl] * dww_ref[k:k + 1, sl]
                            part = term if part is None else part + term
                    if phase == 0:
                        c_s[r:r + cb, sl] = part + dwb_ref[:, sl]
                    else:
                        pad = jnp.zeros((CONV_PHASE_ROWS - n_rows, LANES), F32)
                        zb_s[slot, (phase - 1) * CONV_PHASE_ROWS:phase * CONV_PHASE_ROWS, sl] = (
                            jnp.concatenate([part, pad], axis=0).astype(BF16))
            y = c_s[r:r + cb, :] + jnp.dot(shift_ref[...], zb_s[slot], preferred_element_type=F32)
            mu = jnp.mean(y, axis=-1, keepdims=True)
            yc = y - mu
            yn = yc * lax.rsqrt(jnp.mean(yc * yc, axis=-1, keepdims=True) + EPS)
            u2 = _silu_half(yn * lng_h + lnb_h)
            yb_s[r:r + cb, :] = (u2 * _silu_half(gz_s[r:r + cb, :])).astype(BF16)

    def emit(sc):
        rows = slice(sc * sub, (sc + 1) * sub)
        out = jnp.dot(yb_s[rows, :], wout_ref[...], preferred_element_type=F32)
        x_new = x_ref[0, rows, :] + gate * out
        if final_norm:
            ms = jnp.mean(x_new * x_new, axis=-1, keepdims=True)
            x_new = (x_new * lax.rsqrt(ms + EPS)) * fg_ref[...]
        o_ref[0, rows, :] = x_new

    for sc in range(min(2, n_sub)):
        project(sc)
    for sc in range(n_sub):
        if sc + 2 < n_sub:
            project(sc + 2)
        mix(sc)
        emit(sc)


def _ada_kernel(c_ref, w_ref, b_ref, o_ref):
    c = c_ref[...]
    s = c * jax.nn.sigmoid(c)
    o_ref[0] = jnp.dot(s, w_ref[0], precision=lax.Precision.HIGHEST,
                       preferred_element_type=F32) + b_ref[0]


def _ada_all(cond, ada_w, ada_b):
    n_layers = ada_w.shape[0]
    tn = 768
    return pl.pallas_call(
        _ada_kernel,
        grid=(n_layers, 3 * D_MODEL // tn),
        in_specs=[pl.BlockSpec((N_COND_ROWS, D_MODEL), lambda l, n: (0, 0)),
                  pl.BlockSpec((1, D_MODEL, tn), lambda l, n: (l, 0, n)),
                  pl.BlockSpec((1, 1, tn), lambda l, n: (l, 0, n))],
        out_specs=pl.BlockSpec((1, N_COND_ROWS, tn), lambda l, n: (l, 0, n)),
        out_shape=jax.ShapeDtypeStruct((n_layers, N_COND_ROWS, 3 * D_MODEL), F32),
        compiler_params=pltpu.CompilerParams(dimension_semantics=("arbitrary", "arbitrary"),
                                             vmem_limit_bytes=VMEM_LIMIT),
        name="adaln",
    )(cond, ada_w, ada_b.reshape(n_layers, 1, 3 * D_MODEL))


def _full(shape):
    return pl.BlockSpec(shape, lambda *_: (0,) * len(shape), pipeline_mode=pl.Buffered(1))


def _ap_scratch(tq, halo):
    n_kv = tq + 2 * halo
    return [pltpu.VMEM((n_kv, D_MODEL), BF16),
            pltpu.VMEM((tq, ATTN_DIM), BF16),
            pltpu.VMEM((2, n_kv, KV_DIM), BF16),
            pltpu.VMEM((2, n_kv // ATTN_BLOCK, KV_DIM, ATTN_BLOCK), BF16),
            pltpu.VMEM((tq + 2 * POOL_HALO, POOL_DIM), F32),
            pltpu.VMEM((tq, D_MODEL), F32),
            pltpu.VMEM((tq, D_MODEL), F32)]


def _ap_latent(x, mod, w, sink, rope, kc, vct, tq):
    batch, seq, _ = x.shape
    bpt = tq // WINDOW
    n_blk = seq // WINDOW
    n_ctx = kc.shape[2]
    main = lambda b, i: (b, i, 0)
    prev = lambda b, i: (b, jnp.maximum(i * bpt - 1, 0), 0)
    nxt = lambda b, i: (b, jnp.minimum((i + 1) * bpt, n_blk - 1), 0)
    tmain = lambda b, i: (i, 0)
    tprev = lambda b, i: (jnp.maximum(i * bpt - 1, 0), 0)
    tnxt = lambda b, i: (jnp.minimum((i + 1) * bpt, n_blk - 1), 0)
    cos, sin = rope
    in_specs = [
        pl.BlockSpec(memory_space=pltpu.SMEM),
        pl.BlockSpec((1, tq, D_MODEL), main),
        pl.BlockSpec((1, WINDOW, D_MODEL), prev),
        pl.BlockSpec((1, WINDOW, D_MODEL), nxt),
        pl.BlockSpec((1, 1, 3 * D_MODEL), lambda b, i: (b, 0, 0)),
        _full((1, D_MODEL)),
        _full((D_MODEL, AP_IN_DIM)), _full((D_MODEL, D_MODEL)),
        _full((len(POOL_WINDOWS), POOL_GROUP, POOL_GROUP)), _full((1, POOL_DIM)),
        pl.BlockSpec((tq, LANES), tmain), pl.BlockSpec((tq, LANES), tmain),
        pl.BlockSpec((WINDOW, LANES), tprev), pl.BlockSpec((WINDOW, LANES), tprev),
        pl.BlockSpec((WINDOW, LANES), tnxt), pl.BlockSpec((WINDOW, LANES), tnxt),
        pl.BlockSpec((1, 2, n_ctx, KV_DIM), lambda b, i: (b, 0, 0, 0)),
        pl.BlockSpec((1, 2, KV_DIM, n_ctx), lambda b, i: (b, 0, 0, 0)),
    ]
    return pl.pallas_call(
        functools.partial(_ap_kernel, tq=tq, seq=seq, latent=True),
        grid=(batch, seq // tq),
        in_specs=in_specs,
        out_specs=pl.BlockSpec((1, tq, D_MODEL), main),
        out_shape=jax.ShapeDtypeStruct(x.shape, F32),
        scratch_shapes=_ap_scratch(tq, WINDOW),
        compiler_params=pltpu.CompilerParams(dimension_semantics=("arbitrary", "arbitrary"),
                                             vmem_limit_bytes=VMEM_LIMIT),
        name="ap_latent",
    )(sink, x, x, x, mod, w["norm_g"], w["win"], w["wout"],
      w["pool_w"], w["pool_scale"], cos, sin, cos, sin, cos, sin, kc, vct)


def _ctx_kv_specs(batch, n):
    kc_spec = pl.BlockSpec((1, 2, n, KV_DIM), lambda b: (b, 0, 0, 0))
    vct_spec = pl.BlockSpec((1, 2, KV_DIM, n), lambda b: (b, 0, 0, 0))
    kc_shape = jax.ShapeDtypeStruct((batch, 2, n, KV_DIM), BF16)
    vct_shape = jax.ShapeDtypeStruct((batch, 2, KV_DIM, n), BF16)
    return [kc_spec, vct_spec], [kc_shape, vct_shape]


def _ap_ctx(xc, mod_ctx, w, sink):
    batch, n, _ = xc.shape
    row = lambda b: (b, 0, 0)
    kv_specs, kv_shapes = _ctx_kv_specs(batch, n)
    in_specs = [
        pl.BlockSpec(memory_space=pltpu.SMEM),
        pl.BlockSpec((1, n, D_MODEL), row),
        _full((1, 1, 3 * D_MODEL)),
        _full((1, D_MODEL)),
        _full((D_MODEL, AP_IN_DIM)), _full((D_MODEL, D_MODEL)),
        _full((len(POOL_WINDOWS), POOL_GROUP, POOL_GROUP)), _full((1, POOL_DIM)),
    ]
    return pl.pallas_call(
        functools.partial(_ap_kernel, tq=n, seq=n, latent=False),
        grid=(batch,),
        in_specs=in_specs,
        out_specs=[pl.BlockSpec((1, n, D_MODEL), row)] + kv_specs,
        out_shape=[jax.ShapeDtypeStruct(xc.shape, F32)] + kv_shapes,
        scratch_shapes=_ap_scratch(n, 0),
        compiler_params=pltpu.CompilerParams(dimension_semantics=("arbitrary",),
                                             vmem_limit_bytes=VMEM_LIMIT),
        name="ap_ctx",
    )(sink, xc, mod_ctx, w["norm_g"], w["win"], w["wout"],
      w["pool_w"], w["pool_scale"])


def _ctx_kv(xc, mod_ctx, w):
    batch, n, _ = xc.shape
    kv_specs, kv_shapes = _ctx_kv_specs(batch, n)
    return pl.pallas_call(
        functools.partial(_ctx_kv_kernel, n_rows=n),
        grid=(batch,),
        in_specs=[pl.BlockSpec((1, n, D_MODEL), lambda b: (b, 0, 0)),
                  _full((1, 1, 3 * D_MODEL)), _full((1, D_MODEL)), _full((D_MODEL, AP_IN_DIM))],
        out_specs=kv_specs,
        out_shape=kv_shapes,
        scratch_shapes=[pltpu.VMEM((n, D_MODEL), BF16)],
        compiler_params=pltpu.CompilerParams(dimension_semantics=("arbitrary",),
                                             vmem_limit_bytes=VMEM_LIMIT),
        name="ctx_kv",
    )(xc, mod_ctx, w["norm_g"], w["win"])


def _cv_layer(x, mod, w, tq, has_halo, final_g=None):
    batch, seq, _ = x.shape
    halo = CONV_HALO
    final_norm = final_g is not None
    if has_halo:
        bpt = tq // halo
        n_blk = seq // halo
        grid = (batch, seq // tq)
        main = lambda b, i: (b, i, 0)
        x_specs = [pl.BlockSpec((1, tq, D_MODEL), main),
                   pl.BlockSpec((1, halo, D_MODEL), lambda b, i: (b, jnp.maximum(i * bpt - 1, 0), 0)),
                   pl.BlockSpec((1, halo, D_MODEL),
                                lambda b, i: (b, jnp.minimum((i + 1) * bpt, n_blk - 1), 0))]
        x_args = [x, x, x]
        mod_spec = pl.BlockSpec((1, 1, 3 * D_MODEL), lambda b, i: (b, 0, 0))
        sem = ("arbitrary", "arbitrary")
    else:
        assert tq == seq
        grid = (batch,)
        main = lambda b: (b, 0, 0)
        x_specs = [pl.BlockSpec((1, tq, D_MODEL), main)]
        x_args = [x]
        mod_spec = _full((1, 1, 3 * D_MODEL))
        sem = ("arbitrary",)
    in_specs = x_specs + [
        mod_spec, _full((1, D_MODEL)),
        _full((D_MODEL, 3 * D_MODEL)), _full((D_MODEL, D_MODEL)),
        _full((CONV_K, D_MODEL)), _full((1, D_MODEL)), _full((1, D_MODEL)), _full((1, D_MODEL)),
        _full((CONV_BLOCK, (SUBLANES - 1) * CONV_PHASE_ROWS)),
    ]
    args = x_args + [mod, w["norm_g"], w["win"], w["wout"], w["dw_w"], w["dw_b"],
                     w["ln_g"], w["ln_b"], _conv_shift_matrix()]
    if final_norm:
        in_specs.append(_full((1, D_MODEL)))
        args.append(final_g)
    scratch = [pltpu.VMEM((tq + 2 * halo, D_MODEL), BF16),
               pltpu.VMEM((tq + 2 * halo, D_MODEL), F32),
               pltpu.VMEM((tq, D_MODEL), F32),
               pltpu.VMEM((tq, D_MODEL), F32),
               pltpu.VMEM((tq, D_MODEL), BF16),
               pltpu.VMEM((2, (SUBLANES - 1) * CONV_PHASE_ROWS, D_MODEL), BF16)]
    return pl.pallas_call(
        functools.partial(_cv_kernel, tq=tq, seq=seq, has_halo=has_halo, final_norm=final_norm),
        grid=grid,
        in_specs=in_specs,
        out_specs=pl.BlockSpec((1, tq, D_MODEL), main),
        out_shape=jax.ShapeDtypeStruct(x.shape, F32),
        scratch_shapes=scratch,
        compiler_params=pltpu.CompilerParams(dimension_semantics=sem,
                                             vmem_limit_bytes=VMEM_LIMIT),
        name="cv_latent" if has_halo else "cv_ctx",
    )(*args)


def _conv_shift_matrix():
    m = np.zeros((CONV_BLOCK, (SUBLANES - 1) * CONV_PHASE_ROWS), np.float32)
    j = np.arange(CONV_BLOCK)
    for p in range(1, SUBLANES):
        m[j, (p - 1) * CONV_PHASE_ROWS + j + p] = 1.0
    return jnp.asarray(m, dtype=BF16)


def _rope_tables(seq):
    rows = seq // GRID_W
    row = jnp.repeat(jnp.arange(rows, dtype=F32), GRID_W)
    col = jnp.tile(jnp.arange(GRID_W, dtype=F32), rows)
    inv_freq = ROPE_THETA ** (-jnp.arange(ROPE_AXIS_PAIRS, dtype=F32) / ROPE_AXIS_PAIRS)
    ang = jnp.concatenate([row[:, None] * inv_freq, col[:, None] * inv_freq], axis=-1)
    cos, sin = jnp.cos(ang), jnp.sin(ang)
    cos_slab = jnp.tile(cos, (1, 4))
    sin_slab = jnp.concatenate([-sin, -sin, sin, sin], axis=-1)
    return cos_slab, sin_slab


def _rope_slab_perm(head_a, head_b):
    half = HEAD_DIM // 2
    cols = []
    for part in range(2):
        for head in (head_a, head_b):
            cols.extend(range(head * HEAD_DIM + part * half, head * HEAD_DIM + (part + 1) * half))
    return cols


def _q_perm():
    return np.asarray(sum((_rope_slab_perm(c, 4 + c) for c in range(4)), []), dtype=np.int32)


def _attn_out_perm():
    perm = []
    for c in range(4):
        for head in (c, 4 + c):
            perm.extend(range(head * HEAD_DIM, (head + 1) * HEAD_DIM))
    return np.asarray(perm, dtype=np.int32)


def _take_runs(w, idx, axis):
    idx = [int(v) for v in idx]
    runs, start = [], idx[0]
    for prev, cur in zip(idx, idx[1:] + [None]):
        if cur != prev + 1:
            runs.append((start, prev + 1))
            start = cur
    return jnp.concatenate([lax.slice_in_dim(w, a, b, axis=axis) for a, b in runs], axis=axis)


def _ap_weights(j, ap_norm_g, ap_w_in, ap_w_out, ap_pool_w, ap_pool_scale):
    o_k = ATTN_DIM
    o_v = ATTN_DIM + KV_DIM
    o_gz = ATTN_DIM + 2 * KV_DIM + POOL_DIM
    cols = np.arange(AP_IN_DIM, dtype=np.int32)
    cols[:o_k] = _q_perm()
    cols[o_k:o_v] = o_k + np.asarray(_rope_slab_perm(0, 1), dtype=np.int32)
    cols[o_gz:o_gz + ATTN_DIM] = o_gz + _attn_out_perm()
    scale = np.ones((AP_IN_DIM,), np.float32)
    scale[:o_k] = HEAD_DIM ** -0.5 * LOG2_E
    scale[o_gz:] = 0.5
    w_in = (_take_runs(ap_w_in[j], cols, 1) * scale).astype(BF16)
    rows = np.arange(D_MODEL, dtype=np.int32)
    rows[:ATTN_DIM] = _attn_out_perm()
    return {
        "norm_g": ap_norm_g[j].reshape(1, D_MODEL),
        "win": w_in,
        "wout": _take_runs(ap_w_out[j], rows, 0).astype(BF16),
        "pool_w": ap_pool_w[j].astype(BF16),
        "pool_scale": ap_pool_scale[j].reshape(1, POOL_DIM),
    }


def _cv_weights(j, cv_norm_g, cv_w_in, cv_w_out, cv_dw_w, cv_dw_b, cv_ln_g, cv_ln_b):
    return {
        "norm_g": cv_norm_g[j].reshape(1, D_MODEL),
        "win": (0.5 * cv_w_in[j]).astype(BF16),
        "wout": cv_w_out[j].astype(BF16),
        "dw_w": cv_dw_w[j],
        "dw_b": cv_dw_b[j].reshape(1, D_MODEL),
        "ln_g": cv_ln_g[j].reshape(1, D_MODEL),
        "ln_b": cv_ln_b[j].reshape(1, D_MODEL),
    }


def kernel(x, c, ctx, c_ctx, ap_ada_w, ap_ada_b, ap_norm_g, ap_w_in, ap_w_out, ap_sink, ap_pool_w, ap_pool_scale, cv_ada_w, cv_ada_b, cv_norm_g, cv_w_in, cv_w_out, cv_dw_w, cv_dw_b, cv_ln_g, cv_ln_b, final_norm_g):
    batch, seq, _ = x.shape
    cond = jnp.zeros((N_COND_ROWS, D_MODEL), F32).at[:batch].set(c).at[batch].set(c_ctx)
    ap_mod = _ada_all(cond, ap_ada_w, ap_ada_b).reshape(-1, N_COND_ROWS, 1, 3 * D_MODEL)
    cv_mod = _ada_all(cond, cv_ada_w, cv_ada_b).reshape(-1, N_COND_ROWS, 1, 3 * D_MODEL)
    rope = _rope_tables(seq)
    last_ap_layer = ((DEPTH - 1) // 2) * 2
    xc = ctx
    for i in range(DEPTH):
        j = i // 2
        update_ctx = i < last_ap_layer
        if i % 2 == 0:
            w = _ap_weights(j, ap_norm_g, ap_w_in, ap_w_out, ap_pool_w, ap_pool_scale)
            mod = ap_mod[j]
            mod_ctx = mod[batch:batch + 1]
            if update_ctx:
                xc, kc, vct = _ap_ctx(xc, mod_ctx, w, ap_sink[j])
            else:
                kc, vct = _ctx_kv(xc, mod_ctx, w)
            x = _ap_latent(x, mod, w, ap_sink[j], rope, kc, vct, AP_TILE_ROWS)
        else:
            w = _cv_weights(j, cv_norm_g, cv_w_in, cv_w_out, cv_dw_w, cv_dw_b, cv_ln_g, cv_ln_b)
            mod = cv_mod[j]
            if update_ctx:
                xc = _cv_layer(xc, mod[batch:batch + 1], w, xc.shape[1], has_halo=False)
            final_g = final_norm_g.reshape(1, D_MODEL) if i == DEPTH - 1 else None
            x = _cv_layer(x, mod, w, CV_TILE_ROWS, has_halo=True, final_g=final_g)
    return x
```

```python
import functools

import jax
import jax.numpy as jnp
import numpy as np
from jax import lax
from jax.experimental import pallas as pl
from jax.experimental.pallas import tpu as pltpu

F32 = jnp.float32
BF16 = jnp.bfloat16

D_MODEL = 1024
DEPTH = 4
GRID_W = 64
HEAD_DIM = 64
ATTN_DIM = 512
N_Q_HEADS = 8
KV_DIM = 128
WINDOW = 128
ATTN_BLOCK = 128
ROPE_THETA = 10000.0
ROPE_AXIS_PAIRS = HEAD_DIM // 4
POOL_WINDOWS = (2, 4, 8, 16)
POOL_DIM = 512
POOL_GROUP = 128
POOL_HALO = 8
CONV_K = 31
CONV_HALO = 16
AP_TILE_ROWS = 1024
CV_TILE_ROWS = 1024
CV_SUB_ROWS = 256
CONV_BLOCK = 128
CONV_PHASE_ROWS = 144
AP_IN_DIM = ATTN_DIM + 2 * KV_DIM + POOL_DIM + D_MODEL
AP_SUB_ROWS = 256
POOL_CHUNK = 64
EPS = 1e-6
NEG_INF = -1e30
LOG2_E = 1.4426950408889634

LANES = 128
SUBLANES = 8
BF16_ROWS = 16
N_COND_ROWS = 16
VMEM_LIMIT = 56 * 1024 * 1024
NT_DIMS = (((1,), (1,)), ((), ()))


def _gated_half(ha, hb):
    return ha + ha * jnp.tanh(hb)


def _silu_half(h):
    return _gated_half(h, h)


def _modnorm(x, gain, shift):
    ms = jnp.mean(x * x, axis=-1, keepdims=True)
    return (x * lax.rsqrt(ms + EPS)) * gain + shift


def _modnorm_rows(dst_ref, dst_off, src_ref, n_rows, chunk, gain, shift):
    def body(j, carry):
        r = pl.multiple_of(j * chunk, chunk)
        x = src_ref[0, pl.ds(r, chunk), :]
        dst_ref[pl.ds(pl.multiple_of(dst_off + r, BF16_ROWS), chunk), :] = (
            _modnorm(x, gain, shift).astype(BF16))
        return carry

    lax.fori_loop(0, n_rows // chunk, body, 0)


def _mod_params(mod_ref, g_ref):
    shift = mod_ref[0, :, 0:D_MODEL]
    gain = g_ref[...] * (1.0 + mod_ref[0, :, D_MODEL:2 * D_MODEL])
    gate = mod_ref[0, :, 2 * D_MODEL:3 * D_MODEL]
    return shift, gain, gate


def _rope(x, cos, sin):
    return x * cos + pltpu.roll(x, 64, 1) * sin


def _head_a_lanes(rows):
    lane = lax.broadcasted_iota(jnp.int32, (rows, LANES), 1)
    return (lane & 32) == 0


def _scores(q_stack, k_blocks, masks):
    scores = []
    for k, mask in zip(k_blocks, masks):
        s = lax.dot_general(k, q_stack, NT_DIMS, preferred_element_type=F32)
        if mask is not None:
            s = jnp.where(mask, s, NEG_INF)
        scores.append(s)
    return scores


def _softmax(scores, sink_row):
    m = sink_row
    for s in scores:
        m = jnp.maximum(m, jnp.max(s, axis=0, keepdims=True))
    denom = jnp.exp2(sink_row - m)
    probs = []
    for s in scores:
        p = jnp.exp2(s - m)
        denom = denom + jnp.sum(p, axis=0, keepdims=True)
        probs.append(p.astype(BF16))
    return probs, denom


def _sink_row(sink_ref, kv_head):
    return jnp.concatenate(
        [jnp.full((1, LANES), sink_ref[c + 4 * kv_head] * LOG2_E, F32) for c in range(4)], axis=1)


def _ap_kernel(*refs, tq, seq, latent):
    if latent:
        (sink_ref, x_ref, xp_ref, xn_ref, mod_ref, g_ref, win_ref,
         wout_ref, pw_ref, ps_ref, cos_ref, sin_ref, cosp_ref, sinp_ref, cosn_ref, sinn_ref,
         kc_ref, vct_ref, o_ref,
         h_s, q_s, kl_s, vt_s, u_s, gz_s, y_s) = refs
        halo = WINDOW
        i = pl.program_id(1)
        n_tiles = seq // tq
    else:
        (sink_ref, x_ref, mod_ref, g_ref, win_ref,
         wout_ref, pw_ref, ps_ref, o_ref, kc_out, vct_out,
         h_s, q_s, kl_s, vt_s, u_s, gz_s, y_s) = refs
        halo = 0
        i = 0
        n_tiles = 1
    t0 = i * tq
    wq_ref = win_ref.at[:, 0:ATTN_DIM]
    wkv_ref = win_ref.at[:, ATTN_DIM:ATTN_DIM + 2 * KV_DIM]
    wu_ref = win_ref.at[:, ATTN_DIM + 2 * KV_DIM:ATTN_DIM + 2 * KV_DIM + POOL_DIM]
    wgz_ref = win_ref.at[:, ATTN_DIM + 2 * KV_DIM + POOL_DIM:AP_IN_DIM]
    n_blocks = tq // ATTN_BLOCK
    halo_blocks = halo // ATTN_BLOCK
    sub = AP_SUB_ROWS
    n_sub = tq // sub
    blocks_per_sub = sub // ATTN_BLOCK

    shift, gain, gate = _mod_params(mod_ref, g_ref)
    head_a = _head_a_lanes(ATTN_BLOCK)
    dim_a = lax.broadcasted_iota(jnp.int32, (KV_DIM, ATTN_BLOCK), 0) < HEAD_DIM

    def norm_rows(dst_lo, src_ref, src_lo, n):
        h_s[dst_lo:dst_lo + n, :] = _modnorm(src_ref[0, src_lo:src_lo + n, :], gain,
                                             shift).astype(BF16)

    def project_kv(h_lo, n, c_ref, s_ref, t_lo):
        z = jnp.dot(h_s[h_lo:h_lo + n, :], wkv_ref[...], preferred_element_type=F32)
        for b in range(n // ATTN_BLOCK):
            zr = slice(b * ATTN_BLOCK, (b + 1) * ATTN_BLOCK)
            hr = slice(h_lo + b * ATTN_BLOCK, h_lo + (b + 1) * ATTN_BLOCK)
            tr = slice(t_lo + b * ATTN_BLOCK, t_lo + (b + 1) * ATTN_BLOCK)
            k = z[zr, 0:LANES]
            if latent:
                k = _rope(k, c_ref[tr, :], s_ref[tr, :])
            kl_s[0, hr, :] = jnp.where(head_a, k, 0.0).astype(BF16)
            kl_s[1, hr, :] = jnp.where(head_a, 0.0, k).astype(BF16)
            vt = z[zr, LANES:2 * LANES].T
            blk = h_lo // ATTN_BLOCK + b
            vt_s[0, blk] = jnp.where(dim_a, vt, 0.0).astype(BF16)
            vt_s[1, blk] = jnp.where(dim_a, 0.0, vt).astype(BF16)

    def project_main(sc):
        rows = slice(sc * sub, (sc + 1) * sub)
        hm = h_s[halo + sc * sub:halo + (sc + 1) * sub, :]
        u_s[POOL_HALO + sc * sub:POOL_HALO + (sc + 1) * sub, :] = jnp.dot(
            hm, wu_ref[...], preferred_element_type=F32)
        zq = jnp.dot(hm, wq_ref[...], preferred_element_type=F32)
        for c in range(4):
            sl = slice(c * LANES, (c + 1) * LANES)
            q = zq[:, sl]
            if latent:
                q = _rope(q, cos_ref[rows, :], sin_ref[rows, :])
            q_s[rows, sl] = q.astype(BF16)
        gz_s[rows, :] = jnp.dot(hm, wgz_ref[...], preferred_element_type=F32)

    def pool_halos():
        zero_halo = jnp.zeros((POOL_HALO, POOL_DIM), F32)
        if latent:
            up = jnp.dot(h_s[halo - BF16_ROWS:halo, :], wu_ref[...], preferred_element_type=F32)
            un = jnp.dot(h_s[halo + tq:halo + tq + BF16_ROWS, :], wu_ref[...],
                         preferred_element_type=F32)
            u_s[0:POOL_HALO, :] = jnp.where(i > 0, up[BF16_ROWS - POOL_HALO:BF16_ROWS], zero_halo)
            u_s[POOL_HALO + tq:2 * POOL_HALO + tq, :] = jnp.where(i < n_tiles - 1,
                                                                   un[0:POOL_HALO], zero_halo)
        else:
            u_s[0:POOL_HALO, :] = zero_halo
            u_s[POOL_HALO + tq:2 * POOL_HALO + tq, :] = zero_halo

    def pool(sc):
        win_rows = POOL_CHUNK + 2 * POOL_HALO
        diffs = []
        for r in range(sc * sub, (sc + 1) * sub, POOL_CHUNK):
            row = []
            for gi, w in enumerate(POOL_WINDOWS):
                sl = slice(gi * POOL_GROUP, (gi + 1) * POOL_GROUP)
                win = u_s[r:r + win_rows, sl]
                acc = win + pltpu.roll(win, 1, 0)
                span = 2
                while span < w:
                    acc = pltpu.roll(acc, span // 2, 0) + pltpu.roll(acc, win_rows - span // 2, 0)
                    span *= 2
                acc = acc[POOL_HALO:POOL_HALO + POOL_CHUNK]
                if 0 < r < tq - POOL_CHUNK:
                    mean = acc * (1.0 / w)
                else:
                    t = t0 + r + lax.broadcasted_iota(jnp.int32, (POOL_CHUNK, POOL_GROUP), 0)
                    cnt = (jnp.minimum(t + w // 2, seq) - jnp.maximum(t - w // 2, 0)).astype(F32)
                    mean = acc / cnt
                diff = mean - win[POOL_HALO:POOL_HALO + POOL_CHUNK]
                row.append(diff.astype(BF16))
            diffs.append(row)
        for gi in range(len(POOL_WINDOWS)):
            sl = slice(gi * POOL_GROUP, (gi + 1) * POOL_GROUP)
            d = jnp.concatenate([row[gi] for row in diffs], axis=0)
            yg = jnp.dot(d, pw_ref[gi], preferred_element_type=F32) * ps_ref[:, sl]
            y_s[sc * sub:(sc + 1) * sub, ATTN_DIM + gi * POOL_GROUP:ATTN_DIM + (gi + 1) * POOL_GROUP] = yg

    if latent:
        kk = lax.broadcasted_iota(jnp.int32, (ATTN_BLOCK, 4 * ATTN_BLOCK), 0)
        qq = lax.broadcasted_iota(jnp.int32, (ATTN_BLOCK, 4 * ATTN_BLOCK), 1) & (ATTN_BLOCK - 1)
        no_prev = jnp.where(i > 0, 0, ATTN_BLOCK)
        no_next = jnp.where(i < n_tiles - 1, 0, ATTN_BLOCK)

    out_a = lax.broadcasted_iota(jnp.int32, (KV_DIM, 4 * ATTN_BLOCK), 0) < HEAD_DIM

    def block_scores(j):
        r = j * ATTN_BLOCK
        q_stack = jnp.concatenate([q_s[r:r + ATTN_BLOCK, c * LANES:(c + 1) * LANES] for c in range(4)],
                                  axis=0)
        if latent:
            prev_ok = kk >= (qq + no_prev if j == 0 else qq)
            next_ok = kk <= (qq - no_next if j == n_blocks - 1 else qq)
            masks = [prev_ok, None, next_ok, None]
        else:
            masks = [None]
        out = []
        for kvh in range(2):
            if latent:
                k_blocks = [kl_s[kvh, r + b * ATTN_BLOCK:r + (b + 1) * ATTN_BLOCK, :] for b in range(3)]
                k_blocks.append(kc_ref[0, kvh])
            else:
                k_blocks = [kl_s[kvh]]
            out.append(_scores(q_stack, k_blocks, masks))
        return out

    def block_finish(j, scores):
        r = j * ATTN_BLOCK
        probs, vts, denoms = [], [], []
        for kvh in range(2):
            p, denom = _softmax(scores[kvh], _sink_row(sink_ref, kvh))
            probs.extend(p)
            denoms.append(denom)
            if latent:
                vts.extend([vt_s[kvh, j + b] for b in range(3)] + [vct_ref[0, kvh]])
            else:
                vts.extend([vt_s[kvh, b] for b in range(n_blocks)])
        acc = jnp.dot(jnp.concatenate(vts, axis=1), jnp.concatenate(probs, axis=0),
                      preferred_element_type=F32)
        acc = acc / jnp.where(out_a, denoms[0], denoms[1])
        for c in range(4):
            y_s[r:r + ATTN_BLOCK, c * LANES:(c + 1) * LANES] = acc[:, c * ATTN_BLOCK:(c + 1) * ATTN_BLOCK].T

    def emit(lo, n):
        rows = slice(lo, lo + n)
        yb = (y_s[rows, :] * _silu_half(gz_s[rows, :])).astype(BF16)
        out = jnp.dot(yb, wout_ref[...], preferred_element_type=F32)
        o_ref[0, rows, :] = x_ref[0, rows, :] + gate * out

    if latent:
        norm_rows(0, xp_ref, 0, halo)
    norm_rows(halo, x_ref, 0, sub)
    if latent:
        project_kv(0, halo, cosp_ref, sinp_ref, 0)
    for sc in range(n_sub):
        if sc + 1 < n_sub:
            norm_rows(halo + (sc + 1) * sub, x_ref, (sc + 1) * sub, sub)
        elif latent:
            norm_rows(halo + tq, xn_ref, 0, halo)
        project_kv(halo + sc * sub, sub, cos_ref if latent else None, sin_ref if latent else None,
                   sc * sub)
        project_main(sc)
        if sc == 0:
            assert blocks_per_sub >= 2 or n_sub == 1
            scores = block_scores(0)
    if latent:
        project_kv(halo + tq, halo, cosn_ref, sinn_ref, 0)
    pool_halos()
    if not latent:
        kc_out[0] = kl_s[...]
        for kvh in range(2):
            for b in range(n_blocks):
                vct_out[0, kvh, :, b * ATTN_BLOCK:(b + 1) * ATTN_BLOCK] = vt_s[kvh, b]

    for j in range(n_blocks):
        sc, first_of_sub = divmod(j, blocks_per_sub)
        if first_of_sub == 0:
            pool(sc)
        nxt = block_scores(j + 1) if j + 1 < n_blocks else None
        block_finish(j, scores)
        scores = nxt
        if (j + 1) % blocks_per_sub == 0:
            emit(sc * sub, sub)


def _ctx_kv_kernel(x_ref, mod_ref, g_ref, win_ref, kc_out, vct_out, h_s, *, n_rows):
    shift, gain, _ = _mod_params(mod_ref, g_ref)
    _modnorm_rows(h_s, 0, x_ref, n_rows, 128, gain, shift)
    z = jnp.dot(h_s[...], win_ref[:, ATTN_DIM:ATTN_DIM + 2 * KV_DIM], preferred_element_type=F32)
    head_a = _head_a_lanes(n_rows)
    k = z[:, 0:LANES]
    kc_out[0, 0] = jnp.where(head_a, k, 0.0).astype(BF16)
    kc_out[0, 1] = jnp.where(head_a, 0.0, k).astype(BF16)
    dim_a = lax.broadcasted_iota(jnp.int32, (KV_DIM, ATTN_BLOCK), 0) < HEAD_DIM
    for b in range(n_rows // ATTN_BLOCK):
        cols = slice(b * ATTN_BLOCK, (b + 1) * ATTN_BLOCK)
        vt = z[cols, LANES:2 * LANES].T
        vct_out[0, 0, :, cols] = jnp.where(dim_a, vt, 0.0).astype(BF16)
        vct_out[0, 1, :, cols] = jnp.where(dim_a, 0.0, vt).astype(BF16)


def _cv_kernel(*refs, tq, seq, has_halo, final_norm):
    refs = list(refs)
    x_ref = refs.pop(0)
    if has_halo:
        xp_ref = refs.pop(0)
        xn_ref = refs.pop(0)
    (mod_ref, g_ref, win_ref, wout_ref, dww_ref, dwwb_ref, dwb_ref, lng_ref, lnb_ref,
     shift_ref) = refs[:10]
    refs = refs[10:]
    wab_ref = win_ref.at[:, 0:2 * D_MODEL]
    wgz_ref = win_ref.at[:, 2 * D_MODEL:3 * D_MODEL]
    if final_norm:
        fg_ref = refs.pop(0)
    o_ref, h_s, u_s, ub0_s, ub8_s, gz_s, yb_s, zb_s, c_s = refs
    halo = CONV_HALO
    if has_halo:
        i = pl.program_id(1)
        n_tiles = seq // tq
    sub = CV_SUB_ROWS
    n_sub = tq // sub
    cb = CONV_BLOCK

    shift, gain, gate = _mod_params(mod_ref, g_ref)
    lng_h = 0.5 * lng_ref[...]
    lnb_h = 0.5 * lnb_ref[...]
    zero_halo = jnp.zeros((halo, D_MODEL), F32)
    u_s[tq + 2 * halo:tq + 3 * halo, :] = zero_halo
    if not has_halo:
        u_s[0:halo, :] = zero_halo
        u_s[halo + tq:2 * halo + tq, :] = zero_halo
    ub_rows_done = [0]

    def to_bf16(upto):
        lo = ub_rows_done[0]
        if upto > lo:
            ub0_s[lo:upto, :] = u_s[lo:upto, :].astype(BF16)
            ub8_s[lo:upto, :] = u_s[lo + SUBLANES:upto + SUBLANES, :].astype(BF16)
            ub_rows_done[0] = upto

    def project(sc):
        lo = halo + sc * sub
        hi = lo + sub
        h_s[lo:hi, :] = _modnorm(x_ref[0, sc * sub:(sc + 1) * sub, :], gain, shift).astype(BF16)
        zlo, zhi = lo, hi
        if has_halo and sc == 0:
            h_s[0:halo, :] = _modnorm(xp_ref[0], gain, shift).astype(BF16)
            zlo = 0
        if has_halo and sc == n_sub - 1:
            h_s[halo + tq:2 * halo + tq, :] = _modnorm(xn_ref[0], gain, shift).astype(BF16)
            zhi = tq + 2 * halo
        zab = jnp.dot(h_s[zlo:zhi, :], wab_ref[...], preferred_element_type=F32)
        u_s[zlo:zhi, :] = _gated_half(zab[:, 0:D_MODEL], zab[:, D_MODEL:2 * D_MODEL])
        if has_halo and sc == 0:
            u_s[0:halo, :] = jnp.where(i > 0, u_s[0:halo, :], 0.0)
        if has_halo and sc == n_sub - 1:
            u_s[halo + tq:2 * halo + tq, :] = jnp.where(i < n_tiles - 1,
                                                        u_s[halo + tq:2 * halo + tq, :], 0.0)
        gz_s[sc * sub:(sc + 1) * sub, :] = jnp.dot(h_s[lo:hi, :], wgz_ref[...],
                                                   preferred_element_type=F32)

    def mix(sc):
        to_bf16((sc + 1) * sub + 2 * halo)
        for r in range(sc * sub, (sc + 1) * sub, cb):
            slot = (r // cb) % 2
            for c in range(D_MODEL // LANES):
                sl = slice(c * LANES, (c + 1) * LANES)
                for phase in range(SUBLANES):
                    part = None
                    for a in range(CONV_K // SUBLANES + 1):
                        k = a * SUBLANES + phase - 1
                        if not 0 <= k < CONV_K:
                            continue
                        if phase == 0:
                            lo = r + a * SUBLANES
                            term = u_s[lo:lo + cb, sl] * dww_ref[k:k + 1, sl]
                        else:
                            src = ub8_s if a % 2 else ub0_s
                            lo = r + BF16_ROWS * (a // 2)
                            term = src[lo:lo + CONV_PHASE_ROWS, sl] * dwwb_ref[k:k + 1, sl]
                        part = term if part is None else part + term
                    if phase == 0:
                        c_s[slot, :, sl] = part + dwb_ref[:, sl]
                    else:
                        zb_s[slot, (phase - 1) * CONV_PHASE_ROWS:phase * CONV_PHASE_ROWS, sl] = part
            y = c_s[slot] + jnp.dot(shift_ref[...], zb_s[slot], preferred_element_type=F32)
            mu = jnp.mean(y, axis=-1, keepdims=True)
            yc = y - mu
            yn = yc * lax.rsqrt(jnp.mean(yc * yc, axis=-1, keepdims=True) + EPS)
            u2 = _silu_half(yn * lng_h + lnb_h)
            yb_s[r:r + cb, :] = (u2 * _silu_half(gz_s[r:r + cb, :])).astype(BF16)

    def emit(sc):
        rows = slice(sc * sub, (sc + 1) * sub)
        out = jnp.dot(yb_s[rows, :], wout_ref[...], preferred_element_type=F32)
        x_new = x_ref[0, rows, :] + gate * out
        if final_norm:
            ms = jnp.mean(x_new * x_new, axis=-1, keepdims=True)
            x_new = (x_new * lax.rsqrt(ms + EPS)) * fg_ref[...]
        o_ref[0, rows, :] = x_new

    for sc in range(min(2, n_sub)):
        project(sc)
    for sc in range(n_sub):
        if sc + 2 < n_sub:
            project(sc + 2)
        mix(sc)
        emit(sc)


def _ada_kernel(c_ref, w_ref, b_ref, o_ref):
    c = c_ref[...]
    s = c * jax.nn.sigmoid(c)
    o_ref[0] = jnp.dot(s, w_ref[0], precision=lax.Precision.HIGHEST,
                       preferred_element_type=F32) + b_ref[0]


def _ada_all(cond, ada_w, ada_b):
    n_layers = ada_w.shape[0]
    tn = 768
    return pl.pallas_call(
        _ada_kernel,
        grid=(n_layers, 3 * D_MODEL // tn),
        in_specs=[pl.BlockSpec((N_COND_ROWS, D_MODEL), lambda l, n: (0, 0)),
                  pl.BlockSpec((1, D_MODEL, tn), lambda l, n: (l, 0, n)),
                  pl.BlockSpec((1, 1, tn), lambda l, n: (l, 0, n))],
        out_specs=pl.BlockSpec((1, N_COND_ROWS, tn), lambda l, n: (l, 0, n)),
        out_shape=jax.ShapeDtypeStruct((n_layers, N_COND_ROWS, 3 * D_MODEL), F32),
        compiler_params=pltpu.CompilerParams(dimension_semantics=("arbitrary", "arbitrary"),
                                             vmem_limit_bytes=VMEM_LIMIT),
        name="adaln",
    )(cond, ada_w, ada_b.reshape(n_layers, 1, 3 * D_MODEL))


def _full(shape):
    return pl.BlockSpec(shape, lambda *_: (0,) * len(shape), pipeline_mode=pl.Buffered(1))


def _ap_scratch(tq, halo):
    n_kv = tq + 2 * halo
    return [pltpu.VMEM((n_kv, D_MODEL), BF16),
            pltpu.VMEM((tq, ATTN_DIM), BF16),
            pltpu.VMEM((2, n_kv, KV_DIM), BF16),
            pltpu.VMEM((2, n_kv // ATTN_BLOCK, KV_DIM, ATTN_BLOCK), BF16),
            pltpu.VMEM((tq + 2 * POOL_HALO, POOL_DIM), F32),
            pltpu.VMEM((tq, D_MODEL), F32),
            pltpu.VMEM((tq, D_MODEL), F32)]


def _ap_latent(x, mod, w, sink, rope, kc, vct, tq):
    batch, seq, _ = x.shape
    bpt = tq // WINDOW
    n_blk = seq // WINDOW
    n_ctx = kc.shape[2]
    main = lambda b, i: (b, i, 0)
    prev = lambda b, i: (b, jnp.maximum(i * bpt - 1, 0), 0)
    nxt = lambda b, i: (b, jnp.minimum((i + 1) * bpt, n_blk - 1), 0)
    tmain = lambda b, i: (i, 0)
    tprev = lambda b, i: (jnp.maximum(i * bpt - 1, 0), 0)
    tnxt = lambda b, i: (jnp.minimum((i + 1) * bpt, n_blk - 1), 0)
    cos, sin = rope
    in_specs = [
        pl.BlockSpec(memory_space=pltpu.SMEM),
        pl.BlockSpec((1, tq, D_MODEL), main),
        pl.BlockSpec((1, WINDOW, D_MODEL), prev),
        pl.BlockSpec((1, WINDOW, D_MODEL), nxt),
        pl.BlockSpec((1, 1, 3 * D_MODEL), lambda b, i: (b, 0, 0)),
        _full((1, D_MODEL)),
        _full((D_MODEL, AP_IN_DIM)), _full((D_MODEL, D_MODEL)),
        _full((len(POOL_WINDOWS), POOL_GROUP, POOL_GROUP)), _full((1, POOL_DIM)),
        pl.BlockSpec((tq, LANES), tmain), pl.BlockSpec((tq, LANES), tmain),
        pl.BlockSpec((WINDOW, LANES), tprev), pl.BlockSpec((WINDOW, LANES), tprev),
        pl.BlockSpec((WINDOW, LANES), tnxt), pl.BlockSpec((WINDOW, LANES), tnxt),
        pl.BlockSpec((1, 2, n_ctx, KV_DIM), lambda b, i: (b, 0, 0, 0)),
        pl.BlockSpec((1, 2, KV_DIM, n_ctx), lambda b, i: (b, 0, 0, 0)),
    ]
    return pl.pallas_call(
        functools.partial(_ap_kernel, tq=tq, seq=seq, latent=True),
        grid=(batch, seq // tq),
        in_specs=in_specs,
        out_specs=pl.BlockSpec((1, tq, D_MODEL), main),
        out_shape=jax.ShapeDtypeStruct(x.shape, F32),
        scratch_shapes=_ap_scratch(tq, WINDOW),
        compiler_params=pltpu.CompilerParams(dimension_semantics=("arbitrary", "arbitrary"),
                                             vmem_limit_bytes=VMEM_LIMIT),
        name="ap_latent",
    )(sink, x, x, x, mod, w["norm_g"], w["win"], w["wout"],
      w["pool_w"], w["pool_scale"], cos, sin, cos, sin, cos, sin, kc, vct)


def _ctx_kv_specs(batch, n):
    kc_spec = pl.BlockSpec((1, 2, n, KV_DIM), lambda b: (b, 0, 0, 0))
    vct_spec = pl.BlockSpec((1, 2, KV_DIM, n), lambda b: (b, 0, 0, 0))
    kc_shape = jax.ShapeDtypeStruct((batch, 2, n, KV_DIM), BF16)
    vct_shape = jax.ShapeDtypeStruct((batch, 2, KV_DIM, n), BF16)
    return [kc_spec, vct_spec], [kc_shape, vct_shape]


def _ap_ctx(xc, mod_ctx, w, sink):
    batch, n, _ = xc.shape
    row = lambda b: (b, 0, 0)
    kv_specs, kv_shapes = _ctx_kv_specs(batch, n)
    in_specs = [
        pl.BlockSpec(memory_space=pltpu.SMEM),
        pl.BlockSpec((1, n, D_MODEL), row),
        _full((1, 1, 3 * D_MODEL)),
        _full((1, D_MODEL)),
        _full((D_MODEL, AP_IN_DIM)), _full((D_MODEL, D_MODEL)),
        _full((len(POOL_WINDOWS), POOL_GROUP, POOL_GROUP)), _full((1, POOL_DIM)),
    ]
    return pl.pallas_call(
        functools.partial(_ap_kernel, tq=n, seq=n, latent=False),
        grid=(batch,),
        in_specs=in_specs,
        out_specs=[pl.BlockSpec((1, n, D_MODEL), row)] + kv_specs,
        out_shape=[jax.ShapeDtypeStruct(xc.shape, F32)] + kv_shapes,
        scratch_shapes=_ap_scratch(n, 0),
        compiler_params=pltpu.CompilerParams(dimension_semantics=("arbitrary",),
                                             vmem_limit_bytes=VMEM_LIMIT),
        name="ap_ctx",
    )(sink, xc, mod_ctx, w["norm_g"], w["win"], w["wout"],
      w["pool_w"], w["pool_scale"])


def _ctx_kv(xc, mod_ctx, w):
    batch, n, _ = xc.shape
    kv_specs, kv_shapes = _ctx_kv_specs(batch, n)
    return pl.pallas_call(
        functools.partial(_ctx_kv_kernel, n_rows=n),
        grid=(batch,),
        in_specs=[pl.BlockSpec((1, n, D_MODEL), lambda b: (b, 0, 0)),
                  _full((1, 1, 3 * D_MODEL)), _full((1, D_MODEL)), _full((D_MODEL, AP_IN_DIM))],
        out_specs=kv_specs,
        out_shape=kv_shapes,
        scratch_shapes=[pltpu.VMEM((n, D_MODEL), BF16)],
        compiler_params=pltpu.CompilerParams(dimension_semantics=("arbitrary",),
                                             vmem_limit_bytes=VMEM_LIMIT),
        name="ctx_kv",
    )(xc, mod_ctx, w["norm_g"], w["win"])


def _cv_layer(x, mod, w, tq, has_halo, final_g=None):
    batch, seq, _ = x.shape
    halo = CONV_HALO
    final_norm = final_g is not None
    if has_halo:
        bpt = tq // halo
        n_blk = seq // halo
        grid = (batch, seq // tq)
        main = lambda b, i: (b, i, 0)
        x_specs = [pl.BlockSpec((1, tq, D_MODEL), main),
                   pl.BlockSpec((1, halo, D_MODEL), lambda b, i: (b, jnp.maximum(i * bpt - 1, 0), 0)),
                   pl.BlockSpec((1, halo, D_MODEL),
                                lambda b, i: (b, jnp.minimum((i + 1) * bpt, n_blk - 1), 0))]
        x_args = [x, x, x]
        mod_spec = pl.BlockSpec((1, 1, 3 * D_MODEL), lambda b, i: (b, 0, 0))
        sem = ("arbitrary", "arbitrary")
    else:
        assert tq == seq
        grid = (batch,)
        main = lambda b: (b, 0, 0)
        x_specs = [pl.BlockSpec((1, tq, D_MODEL), main)]
        x_args = [x]
        mod_spec = _full((1, 1, 3 * D_MODEL))
        sem = ("arbitrary",)
    in_specs = x_specs + [
        mod_spec, _full((1, D_MODEL)),
        _full((D_MODEL, 3 * D_MODEL)), _full((D_MODEL, D_MODEL)),
        _full((CONV_K, D_MODEL)), _full((CONV_K, D_MODEL)),
        _full((1, D_MODEL)), _full((1, D_MODEL)), _full((1, D_MODEL)),
        _full((CONV_BLOCK, (SUBLANES - 1) * CONV_PHASE_ROWS)),
    ]
    args = x_args + [mod, w["norm_g"], w["win"], w["wout"], w["dw_w"], w["dw_w"].astype(BF16), w["dw_b"],
                     w["ln_g"], w["ln_b"], _conv_shift_matrix()]
    if final_norm:
        in_specs.append(_full((1, D_MODEL)))
        args.append(final_g)
    scratch = [pltpu.VMEM((tq + 2 * halo, D_MODEL), BF16),
               pltpu.VMEM((tq + 3 * halo, D_MODEL), F32),
               pltpu.VMEM((tq + 2 * halo, D_MODEL), BF16),
               pltpu.VMEM((tq + 2 * halo, D_MODEL), BF16),
               pltpu.VMEM((tq, D_MODEL), F32),
               pltpu.VMEM((tq, D_MODEL), BF16),
               pltpu.VMEM((2, (SUBLANES - 1) * CONV_PHASE_ROWS, D_MODEL), BF16),
               pltpu.VMEM((2, CONV_BLOCK, D_MODEL), F32)]
    return pl.pallas_call(
        functools.partial(_cv_kernel, tq=tq, seq=seq, has_halo=has_halo, final_norm=final_norm),
        grid=grid,
        in_specs=in_specs,
        out_specs=pl.BlockSpec((1, tq, D_MODEL), main),
        out_shape=jax.ShapeDtypeStruct(x.shape, F32),
        scratch_shapes=scratch,
        compiler_params=pltpu.CompilerParams(dimension_semantics=sem,
                                             vmem_limit_bytes=VMEM_LIMIT),
        name="cv_latent" if has_halo else "cv_ctx",
    )(*args)


def _conv_shift_matrix():
    m = np.zeros((CONV_BLOCK, (SUBLANES - 1) * CONV_PHASE_ROWS), np.float32)
    j = np.arange(CONV_BLOCK)
    for p in range(1, SUBLANES):
        m[j, (p - 1) * CONV_PHASE_ROWS + j + p] = 1.0
    return jnp.asarray(m, dtype=BF16)


def _rope_tables(seq):
    rows = seq // GRID_W
    row = jnp.repeat(jnp.arange(rows, dtype=F32), GRID_W)
    col = jnp.tile(jnp.arange(GRID_W, dtype=F32), rows)
    inv_freq = ROPE_THETA ** (-jnp.arange(ROPE_AXIS_PAIRS, dtype=F32) / ROPE_AXIS_PAIRS)
    ang = jnp.concatenate([row[:, None] * inv_freq, col[:, None] * inv_freq], axis=-1)
    cos, sin = jnp.cos(ang), jnp.sin(ang)
    cos_slab = jnp.tile(cos, (1, 4))
    sin_slab = jnp.concatenate([-sin, -sin, sin, sin], axis=-1)
    return cos_slab, sin_slab


def _rope_slab_perm(head_a, head_b):
    half = HEAD_DIM // 2
    cols = []
    for part in range(2):
        for head in (head_a, head_b):
            cols.extend(range(head * HEAD_DIM + part * half, head * HEAD_DIM + (part + 1) * half))
    return cols


def _q_perm():
    return np.asarray(sum((_rope_slab_perm(c, 4 + c) for c in range(4)), []), dtype=np.int32)


def _attn_out_perm():
    perm = []
    for c in range(4):
        for head in (c, 4 + c):
            perm.extend(range(head * HEAD_DIM, (head + 1) * HEAD_DIM))
    return np.asarray(perm, dtype=np.int32)


def _take_runs(w, idx, axis):
    idx = [int(v) for v in idx]
    runs, start = [], idx[0]
    for prev, cur in zip(idx, idx[1:] + [None]):
        if cur != prev + 1:
            runs.append((start, prev + 1))
            start = cur
    return jnp.concatenate([lax.slice_in_dim(w, a, b, axis=axis) for a, b in runs], axis=axis)


def _ap_weights(j, ap_norm_g, ap_w_in, ap_w_out, ap_pool_w, ap_pool_scale):
    o_k = ATTN_DIM
    o_v = ATTN_DIM + KV_DIM
    o_gz = ATTN_DIM + 2 * KV_DIM + POOL_DIM
    cols = np.arange(AP_IN_DIM, dtype=np.int32)
    cols[:o_k] = _q_perm()
    cols[o_k:o_v] = o_k + np.asarray(_rope_slab_perm(0, 1), dtype=np.int32)
    cols[o_gz:o_gz + ATTN_DIM] = o_gz + _attn_out_perm()
    scale = np.ones((AP_IN_DIM,), np.float32)
    scale[:o_k] = HEAD_DIM ** -0.5 * LOG2_E
    scale[o_gz:] = 0.5
    w_in = (_take_runs(ap_w_in[j], cols, 1) * scale).astype(BF16)
    rows = np.arange(D_MODEL, dtype=np.int32)
    rows[:ATTN_DIM] = _attn_out_perm()
    return {
        "norm_g": ap_norm_g[j].reshape(1, D_MODEL),
        "win": w_in,
        "wout": _take_runs(ap_w_out[j], rows, 0).astype(BF16),
        "pool_w": ap_pool_w[j].astype(BF16),
        "pool_scale": ap_pool_scale[j].reshape(1, POOL_DIM),
    }


def _cv_weights(j, cv_norm_g, cv_w_in, cv_w_out, cv_dw_w, cv_dw_b, cv_ln_g, cv_ln_b):
    return {
        "norm_g": cv_norm_g[j].reshape(1, D_MODEL),
        "win": (0.5 * cv_w_in[j]).astype(BF16),
        "wout": cv_w_out[j].astype(BF16),
        "dw_w": cv_dw_w[j],
        "dw_b": cv_dw_b[j].reshape(1, D_MODEL),
        "ln_g": cv_ln_g[j].reshape(1, D_MODEL),
        "ln_b": cv_ln_b[j].reshape(1, D_MODEL),
    }


def kernel(x, c, ctx, c_ctx, ap_ada_w, ap_ada_b, ap_norm_g, ap_w_in, ap_w_out, ap_sink, ap_pool_w, ap_pool_scale, cv_ada_w, cv_ada_b, cv_norm_g, cv_w_in, cv_w_out, cv_dw_w, cv_dw_b, cv_ln_g, cv_ln_b, final_norm_g):
    batch, seq, _ = x.shape
    cond = jnp.zeros((N_COND_ROWS, D_MODEL), F32).at[:batch].set(c).at[batch].set(c_ctx)
    ap_mod = _ada_all(cond, ap_ada_w, ap_ada_b).reshape(-1, N_COND_ROWS, 1, 3 * D_MODEL)
    cv_mod = _ada_all(cond, cv_ada_w, cv_ada_b).reshape(-1, N_COND_ROWS, 1, 3 * D_MODEL)
    rope = _rope_tables(seq)
    last_ap_layer = ((DEPTH - 1) // 2) * 2
    xc = ctx
    for i in range(DEPTH):
        j = i // 2
        update_ctx = i < last_ap_layer
        if i % 2 == 0:
            w = _ap_weights(j, ap_norm_g, ap_w_in, ap_w_out, ap_pool_w, ap_pool_scale)
            mod = ap_mod[j]
            mod_ctx = mod[batch:batch + 1]
            if update_ctx:
                xc, kc, vct = _ap_ctx(xc, mod_ctx, w, ap_sink[j])
            else:
                kc, vct = _ctx_kv(xc, mod_ctx, w)
            x = _ap_latent(x, mod, w, ap_sink[j], rope, kc, vct, AP_TILE_ROWS)
        else:
            w = _cv_weights(j, cv_norm_g, cv_w_in, cv_w_out, cv_dw_w, cv_dw_b, cv_ln_g, cv_ln_b)
            mod = cv_mod[j]
            if update_ctx:
                xc = _cv_layer(xc, mod[batch:batch + 1], w, xc.shape[1], has_halo=False)
            final_g = final_norm_g.reshape(1, D_MODEL) if i == DEPTH - 1 else None
            x = _cv_layer(x, mod, w, CV_TILE_ROWS, has_halo=True, final_g=final_g)
    return x
```
